```python
import math
import jax, jax.numpy as jnp
from jax import lax
import numpy as np

D_MODEL = 2048
BATCH = 1
SEQ = 8192
DEPTH = 4

D_MIX = D_MODEL
HEAD_DIM = 128
SB_HEADS = 4
GDN_HEADS = 8
SG_GROUPS = 4
SB_W = SB_HEADS * HEAD_DIM
GDN_W = GDN_HEADS * HEAD_DIM
SG_W = SG_GROUPS * HEAD_DIM
SB_BLOCK = 128
GDN_CHUNK = 64
CONV_W = 4
SG_CHUNK = 128
IN_SPLITS = (3 * SB_W,
             3 * SB_W + 3 * GDN_W,
             3 * SB_W + 4 * GDN_W,
             3 * SB_W + 4 * GDN_W + GDN_HEADS,
             3 * SB_W + 4 * GDN_W + 2 * GDN_HEADS,
             3 * SB_W + 4 * GDN_W + 2 * GDN_HEADS + SG_W)
IN_COLS = 3 * SB_W + 4 * GDN_W + 2 * GDN_HEADS + 2 * SG_W
N_EXPERTS = 64
TOP_K = 8
N_GROUPS = 8
TOPK_GROUPS = 4
D_EXPERT = 384
D_SHARED = 384
ROUTED_SCALE = 2.5
ALPHA = (2.0 * DEPTH) ** 0.25
BETA_INIT = (8.0 * DEPTH) ** -0.25
LN_EPS = 1e-5

kernel_name = "hybrid_sb_gdn_sgmlp_moe_deepnorm_adaln"


def _layer_norm(x, w, b):
    xf = x.astype(jnp.float32)
    mu = jnp.mean(xf, axis=-1, keepdims=True)
    var = jnp.mean(jnp.square(xf - mu), axis=-1, keepdims=True)
    return ((xf - mu) * lax.rsqrt(var + LN_EPS)).astype(x.dtype) * w + b


def _l2norm(t):
    tf = t.astype(jnp.float32)
    return tf * lax.rsqrt(jnp.sum(tf * tf, axis=-1, keepdims=True) + 1e-6)


def _to_heads(t, n_heads):
    B, S, _ = t.shape
    return t.reshape(B, S, n_heads, HEAD_DIM).transpose(0, 2, 1, 3)


def _from_heads(t):
    B, H, S, Dh = t.shape
    return t.transpose(0, 2, 1, 3).reshape(B, S, H * Dh)


def _stick_breaking_attention(q, k, v):
    B, H, S, Dh = q.shape
    nb = S // SB_BLOCK
    qf = q.astype(jnp.float32) * (Dh ** -0.5)
    kf = k.astype(jnp.float32)
    vf = v.astype(jnp.float32)
    qb = qf.reshape(B, H, nb, SB_BLOCK, Dh).transpose(2, 0, 1, 3, 4)
    key_pos = jnp.arange(S)

    def block(args):
        i, q_blk = args
        q_pos = i * SB_BLOCK + jnp.arange(SB_BLOCK)
        z = jnp.einsum('bhqd,bhkd->bhqk', q_blk, kf)
        causal = key_pos[None, :] < q_pos[:, None]
        sp = jnp.where(causal, jax.nn.softplus(z), 0.0)
        log_a = z - lax.cumsum(sp, axis=3, reverse=True)
        a = jnp.exp(jnp.where(causal, log_a, -jnp.inf))
        return jnp.einsum('bhqk,bhkd->bhqd', a, vf)

    out = lax.map(block, (jnp.arange(nb), qb))
    return out.transpose(1, 2, 0, 3, 4).reshape(B, H, S, Dh).astype(q.dtype)


def _causal_depthwise_conv(x, w):
    C = x.shape[-1]
    return lax.conv_general_dilated(x, w[:, None, :], window_strides=(1,),
                                    padding=[(CONV_W - 1, 0)],
                                    dimension_numbers=('NWC', 'WIO', 'NWC'),
                                    feature_group_count=C)


def _gated_delta_rule(q, k, v, g, beta):
    B, H, S, Dk = q.shape
    Dv = v.shape[-1]
    C = GDN_CHUNK
    N = S // C
    f32 = jnp.float32
    rs = lambda t: t.astype(f32).reshape(B, H, N, C, *t.shape[3:])
    q, k, v, g, beta = rs(q) * (Dk ** -0.5), rs(k), rs(v), rs(g), rs(beta)
    gc = jnp.cumsum(g, axis=-1)
    idx = jnp.arange(C)
    tril = idx[:, None] >= idx[None, :]
    strict = idx[:, None] > idx[None, :]
    decay = jnp.exp(jnp.where(tril, gc[..., :, None] - gc[..., None, :], -jnp.inf))
    kb = k * beta[..., None]
    lower = jnp.where(strict, jnp.einsum('bhnid,bhnjd->bhnij', kb, k) * decay, 0.0)
    rhs = jnp.concatenate([v * beta[..., None], kb * jnp.exp(gc)[..., None]], axis=-1)
    sol = lax.linalg.triangular_solve(lower + jnp.eye(C, dtype=f32), rhs, left_side=True,
                                      lower=True, unit_diagonal=True)
    u, w = sol[..., :Dv], sol[..., Dv:]
    qk = jnp.where(tril, jnp.einsum('bhnid,bhnjd->bhnij', q, k) * decay, 0.0)
    q_dec = q * jnp.exp(gc)[..., None]
    g_last = gc[..., -1]
    k_dec = k * jnp.exp(g_last[..., None] - gc)[..., None]

    def step(state, xs):
        u_n, w_n, qk_n, qd_n, kd_n, gl_n = xs
        v_new = u_n - jnp.einsum('bhck,bhkv->bhcv', w_n, state)
        o = jnp.einsum('bhck,bhkv->bhcv', qd_n, state) + jnp.einsum('bhij,bhjv->bhiv', qk_n, v_new)
        state = state * jnp.exp(gl_n)[..., None, None] + jnp.einsum('bhck,bhcv->bhkv', kd_n, v_new)
        return state, o

    xs = tuple(jnp.moveaxis(t, 2, 0) for t in (u, w, qk, q_dec, k_dec, g_last))
    _, o = lax.scan(step, jnp.zeros((B, H, Dk, Dv), f32), xs)
    return jnp.moveaxis(o, 0, 2).reshape(B, H, S, Dv)


def _hybrid_mixer(h, w_in, conv_w, gdn_a_log, gdn_dt_bias, gdn_norm_w,
                  sg_norm_w, sg_norm_b, sg_w, sg_b, w_out):
    B, S, _ = h.shape
    proj = jnp.einsum('bsd,de->bse', h, w_in)
    sb_qkv, gdn_qkv, gdn_z, gdn_a, gdn_b, sg_u, sg_v = jnp.split(proj, IN_SPLITS, axis=-1)

    sb_q, sb_k, sb_v = (_to_heads(t, SB_HEADS) for t in jnp.split(sb_qkv, 3, axis=-1))
    y_sb = _from_heads(_stick_breaking_attention(sb_q, sb_k, sb_v))

    gdn_qkv = jax.nn.silu(_causal_depthwise_conv(gdn_qkv, conv_w))
    g_q, g_k, g_v = (_to_heads(t, GDN_HEADS) for t in jnp.split(gdn_qkv, 3, axis=-1))
    beta = jax.nn.sigmoid(gdn_b.astype(jnp.float32)).transpose(0, 2, 1)
    log_decay = (-jnp.exp(gdn_a_log.astype(jnp.float32))
                 * jax.nn.softplus(gdn_a.astype(jnp.float32) + gdn_dt_bias)).transpose(0, 2, 1)
    o = _gated_delta_rule(_l2norm(g_q), _l2norm(g_k), g_v, log_decay, beta)
    o = o * lax.rsqrt(jnp.mean(o * o, axis=-1, keepdims=True) + 1e-6)
    o = o.astype(h.dtype) * gdn_norm_w * jax.nn.silu(_to_heads(gdn_z, GDN_HEADS))
    y_gdn = _from_heads(o)

    u = jax.nn.gelu(sg_u, approximate=False)
    v = jax.nn.gelu(sg_v, approximate=False).reshape(B, S, SG_GROUPS, HEAD_DIM)
    v = _layer_norm(v, sg_norm_w.reshape(SG_GROUPS, HEAD_DIM), sg_norm_b.reshape(SG_GROUPS, HEAD_DIM))
    v = v.reshape(B, S // SG_CHUNK, SG_CHUNK, SG_GROUPS, HEAD_DIM)
    causal = jnp.tril(jnp.ones((SG_CHUNK, SG_CHUNK), dtype=bool))
    w_s = jnp.where(causal, sg_w, 0)
    mixed = jnp.einsum('gts,bnsgd->bntgd', w_s, v) + sg_b.T[:, :, None]
    y_sg = u * mixed.reshape(B, S, SG_W)

    y = jnp.concatenate([y_sb, y_gdn, y_sg], axis=-1)
    return jnp.einsum('bse,ed->bsd', y, w_out)


def _moe(h, w_router, router_bias, exp_w1, exp_w3, exp_w2, sh_w1, sh_w3, sh_w2):
    B, S, D = h.shape
    t = h.reshape(B * S, D)
    T = t.shape[0]
    scores = jax.nn.sigmoid(jnp.einsum('td,de->te', t, w_router).astype(jnp.float32))
    sel = scores + router_bias.astype(jnp.float32)
    grp_score = lax.top_k(sel.reshape(T, N_GROUPS, N_EXPERTS // N_GROUPS), 2)[0].sum(-1)
    _, top_grp = lax.top_k(grp_score, TOPK_GROUPS)
    grp_mask = jnp.sum(jax.nn.one_hot(top_grp, N_GROUPS, dtype=jnp.float32), axis=1)
    sel = jnp.where(jnp.repeat(grp_mask, N_EXPERTS // N_GROUPS, axis=1) > 0, sel, -jnp.inf)
    _, top_idx = lax.top_k(sel, TOP_K)
    w = jnp.take_along_axis(scores, top_idx, axis=1)
    w = w / jnp.sum(w, axis=-1, keepdims=True) * ROUTED_SCALE
    gates = jnp.sum(jax.nn.one_hot(top_idx, N_EXPERTS, dtype=jnp.float32) * w[..., None], axis=1)

    def expert(acc, p):
        e1, e3, e2, g = p
        a = jax.nn.silu(t @ e1) * (t @ e3)
        return acc + (a * g[:, None].astype(a.dtype)) @ e2, None

    routed, _ = lax.scan(expert, jnp.zeros_like(t), (exp_w1, exp_w3, exp_w2, gates.T))
    shared = (jax.nn.silu(t @ sh_w1) * (t @ sh_w3)) @ sh_w2
    return (routed + shared).reshape(B, S, D)


def setup_inputs(seed: int = 0) -> dict:
    key = jax.random.key(seed)
    ks = jax.random.split(key, 28)
    L, D, E, F = DEPTH, D_MODEL, N_EXPERTS, D_EXPERT
    nrm = lambda k, shape, s: jax.random.normal(k, shape, jnp.float32) * s
    dt = jnp.exp(jax.random.uniform(ks[7], (L, GDN_HEADS), jnp.float32,
                                    minval=math.log(1e-3), maxval=math.log(1e-1)))
    return {
        "x": nrm(ks[0], (BATCH, SEQ, D), 1.0),
        "c": nrm(ks[1], (BATCH, D), 1.0),
        "w_ada": nrm(ks[2], (L, D, 6 * D), 0.1 * D ** -0.5),
        "b_ada": nrm(ks[3], (L, 6 * D), 0.02),
        "w_in": nrm(ks[4], (L, D, IN_COLS), D ** -0.5),
        "conv_w": nrm(ks[5], (L, CONV_W, 3 * GDN_W), CONV_W ** -0.5),
        "gdn_a_log": jnp.log(jax.random.uniform(ks[6], (L, GDN_HEADS), jnp.float32, minval=1.0, maxval=16.0)),
        "gdn_dt_bias": dt + jnp.log(-jnp.expm1(-dt)),
        "gdn_norm_w": 1.0 + nrm(ks[8], (L, HEAD_DIM), 0.02),
        "sg_norm_w": 1.0 + nrm(ks[9], (L, SG_W), 0.02),
        "sg_norm_b": nrm(ks[10], (L, SG_W), 0.02),
        "sg_w": nrm(ks[11], (L, SG_GROUPS, SG_CHUNK, SG_CHUNK), SG_CHUNK ** -0.5),
        "sg_b": 1.0 + nrm(ks[12], (L, SG_GROUPS, SG_CHUNK), 0.02),
        "w_out": nrm(ks[13], (L, D_MIX, D), BETA_INIT * D_MIX ** -0.5),
        "ln1_w": 1.0 + nrm(ks[14], (L, D), 0.02),
        "ln1_b": nrm(ks[15], (L, D), 0.02),
        "w_router": nrm(ks[16], (L, D, E), D ** -0.5),
        "router_bias": nrm(ks[17], (L, E), 0.01),
        "exp_w1": nrm(ks[18], (L, E, D, F), D ** -0.5),
        "exp_w3": nrm(ks[19], (L, E, D, F), D ** -0.5),
        "exp_w2": nrm(ks[20], (L, E, F, D), BETA_INIT * F ** -0.5),
        "sh_w1": nrm(ks[21], (L, D, D_SHARED), D ** -0.5),
        "sh_w3": nrm(ks[22], (L, D, D_SHARED), D ** -0.5),
        "sh_w2": nrm(ks[23], (L, D_SHARED, D), BETA_INIT * D_SHARED ** -0.5),
        "ln2_w": 1.0 + nrm(ks[24], (L, D), 0.02),
        "ln2_b": nrm(ks[25], (L, D), 0.02),
    }


def reference(x, c, w_ada, b_ada, w_in, conv_w, gdn_a_log, gdn_dt_bias, gdn_norm_w,
              sg_norm_w, sg_norm_b, sg_w, sg_b, w_out, ln1_w, ln1_b, w_router, router_bias,
              exp_w1, exp_w3, exp_w2, sh_w1, sh_w3, sh_w2, ln2_w, ln2_b):
    c_act = jax.nn.silu(c)
    for l in range(DEPTH):
        mod = jnp.einsum('bd,de->be', c_act, w_ada[l]) + b_ada[l]
        sh1, sc1, gt1, sh2, sc2, gt2 = jnp.split(mod[:, None, :], 6, axis=-1)
        h = x * (1 + sc1) + sh1
        y = _hybrid_mixer(h, w_in[l], conv_w[l], gdn_a_log[l], gdn_dt_bias[l], gdn_norm_w[l],
                          sg_norm_w[l], sg_norm_b[l], sg_w[l], sg_b[l], w_out[l])
        x = _layer_norm(ALPHA * x + (1 + gt1) * y, ln1_w[l], ln1_b[l])
        h = x * (1 + sc2) + sh2
        y = _moe(h, w_router[l], router_bias[l], exp_w1[l], exp_w3[l], exp_w2[l],
                 sh_w1[l], sh_w3[l], sh_w2[l])
        x = _layer_norm(ALPHA * x + (1 + gt2) * y, ln2_w[l], ln2_b[l])
    return x
```

```python
import functools

import jax
import jax.numpy as jnp
from jax import lax
from jax.experimental import pallas as pl
from jax.experimental.pallas import tpu as pltpu

F32 = jnp.float32
BF16 = jnp.bfloat16
I32 = jnp.int32

LANES = 128
SUBLANES = 8
VMEM_LIMIT = 56 * 1024 * 1024

HEAD_DIM = 128
SB_HEADS = 4
GDN_HEADS = 8
SG_GROUPS = 4
SB_W = SB_HEADS * HEAD_DIM
GDN_W = GDN_HEADS * HEAD_DIM
SG_W = SG_GROUPS * HEAD_DIM
GDN_CHUNK = 64
CONV_W = 4
SG_CHUNK = 128
N_EXPERTS = 64
TOP_K = 8
N_GROUPS = 8
GROUP_SIZE = N_EXPERTS // N_GROUPS
TOPK_GROUPS = 4
ROUTED_SCALE = 2.5
LN_EPS = 1e-5

SB_TQ = 256
SB_TK = 256
EXPERT_TM = 256
ROUTER_TM = 512
DISPATCH_TD = 256
COMBINE_TC = 128


def _cparams(sem):
    return pltpu.CompilerParams(dimension_semantics=sem, vmem_limit_bytes=VMEM_LIMIT)


def _sigmoid(x):
    return 1.0 / (1.0 + jnp.exp(-x))


def _silu(x):
    return x * _sigmoid(x)


def _softplus(x):
    return jnp.maximum(x, 0.0) + jnp.log(1.0 + jnp.exp(-jnp.abs(x)))


def _iota(shape, dim):
    return lax.broadcasted_iota(I32, shape, dim)


def _ada_kernel(c_ref, w_ref, b_ref, o_ref):
    cb = _silu(c_ref[...])
    tn = w_ref.shape[2]
    for j in range(tn // LANES):
        sl = slice(j * LANES, (j + 1) * LANES)
        o_ref[0, :, sl] = jnp.sum(w_ref[0, :, sl] * cb, axis=0, keepdims=True) + b_ref[0, :, sl]


def _ada_mod(c, w_ada, b_ada):
    L, D, N = w_ada.shape
    tn = 1536
    cb = jnp.broadcast_to(c.reshape(D, 1), (D, LANES))
    return pl.pallas_call(
        _ada_kernel,
        grid=(L, N // tn),
        in_specs=[pl.BlockSpec((D, LANES), lambda l, j: (0, 0)),
                  pl.BlockSpec((1, D, tn), lambda l, j: (l, 0, j)),
                  pl.BlockSpec((1, 1, tn), lambda l, j: (l, 0, j))],
        out_specs=pl.BlockSpec((1, 1, tn), lambda l, j: (l, 0, j)),
        out_shape=jax.ShapeDtypeStruct((L, 1, N), F32),
        compiler_params=_cparams(("arbitrary", "arbitrary")),
        name="ada_mod",
    )(cb, w_ada, b_ada.reshape(L, 1, N))


def _modulate_kernel(x_ref, sc_ref, sh_ref, o_ref):
    o_ref[...] = (x_ref[...] * (1.0 + sc_ref[...]) + sh_ref[...]).astype(o_ref.dtype)


def _modulate(x, mod, sc_idx, sh_idx):
    T, D = x.shape
    tm = 512
    return pl.pallas_call(
        _modulate_kernel,
        grid=(T // tm,),
        in_specs=[pl.BlockSpec((tm, D), lambda i: (i, 0)),
                  pl.BlockSpec((1, D), lambda i: (0, sc_idx)),
                  pl.BlockSpec((1, D), lambda i: (0, sh_idx))],
        out_specs=pl.BlockSpec((tm, D), lambda i: (i, 0)),
        out_shape=jax.ShapeDtypeStruct((T, D), BF16),
        compiler_params=_cparams(("arbitrary",)),
        name="modulate",
    )(x, mod, mod)


def _proj_kernel(x_ref, w_ref, o_ref, wb_ref):
    @pl.when(pl.program_id(1) == 0)
    def _():
        wb_ref[...] = w_ref[0].astype(BF16)

    o_ref[...] = jnp.dot(x_ref[...], wb_ref[...], preferred_element_type=F32).astype(o_ref.dtype)


def _proj(h, w, layer, col_block0, n_blocks, tn, out_dtype):
    T, K = h.shape
    tm = min(1024, T)
    return pl.pallas_call(
        _proj_kernel,
        grid=(n_blocks, T // tm),
        in_specs=[pl.BlockSpec((tm, K), lambda j, i: (i, 0)),
                  pl.BlockSpec((1, K, tn), lambda j, i: (layer, 0, j + col_block0))],
        out_specs=pl.BlockSpec((tm, tn), lambda j, i: (i, j)),
        out_shape=jax.ShapeDtypeStruct((T, n_blocks * tn), out_dtype),
        scratch_shapes=[pltpu.VMEM((K, tn), BF16)],
        compiler_params=_cparams(("arbitrary", "arbitrary")),
        name="in_proj",
    )(h, w)


def _sb_kernel(q_ref, k_ref, v_ref, o_ref, *, tq, tk, scale):
    i = pl.program_id(1)
    q = q_ref[...]
    qpos = i * tq + _iota((tq, tk), 0)
    kofs = _iota((tq, tk), 1)
    tri = (_iota((tk, tk), 0) >= _iota((tk, tk), 1)).astype(BF16)
    nb = tq // tk

    def body(step, carry):
        acc, r = carry
        j = i * nb + (nb - 1) - step
        ks = pl.multiple_of(j * tk, tk)
        k = k_ref[pl.ds(ks, tk), :]
        v = v_ref[pl.ds(ks, tk), :]
        z = lax.dot_general(q, k, (((1,), (1,)), ((), ())), preferred_element_type=F32) * scale
        causal = (ks + kofs) < qpos
        sp = jnp.where(causal, _softplus(z), 0.0)
        sp_hi = sp.astype(BF16)
        sp_lo = (sp - sp_hi.astype(F32)).astype(BF16)
        c = (jnp.dot(sp_hi, tri, preferred_element_type=F32)
             + jnp.dot(sp_lo, tri, preferred_element_type=F32))
        a = jnp.where(causal, jnp.exp(z - c - r), 0.0)
        acc = acc + jnp.dot(a.astype(BF16), v, preferred_element_type=F32)
        return acc, r + c[:, 0:1]

    acc0 = jnp.zeros((tq, q_ref.shape[1]), F32)
    r0 = jnp.zeros((tq, 1), F32)
    acc, _ = lax.fori_loop(0, (i + 1) * nb, body, (acc0, r0))
    o_ref[...] = acc.astype(o_ref.dtype)


def _sb_attention(qkv):
    T = qkv.shape[0]
    tq, tk = SB_TQ, SB_TK
    kern = functools.partial(_sb_kernel, tq=tq, tk=tk, scale=HEAD_DIM ** -0.5)
    return pl.pallas_call(
        kern,
        grid=(SB_HEADS, T // tq),
        in_specs=[pl.BlockSpec((tq, HEAD_DIM), lambda h, i: (i, h)),
                  pl.BlockSpec((T, HEAD_DIM), lambda h, i: (0, SB_HEADS + h)),
                  pl.BlockSpec((T, HEAD_DIM), lambda h, i: (0, 2 * SB_HEADS + h))],
        out_specs=pl.BlockSpec((tq, HEAD_DIM), lambda h, i: (i, h)),
        out_shape=jax.ShapeDtypeStruct((T, SB_W), BF16),
        compiler_params=_cparams(("arbitrary", "arbitrary")),
        name="sb_attention",
    )(qkv, qkv, qkv)


def _gdn_kernel(qkv_ref, z_ref, ab_ref, cw_ref, alog_ref, dtb_ref, nw_ref, o_ref,
                tail_ref, state_ref):
    C = GDN_CHUNK
    Dh = HEAD_DIM
    step = pl.program_id(0)

    @pl.when(step == 0)
    def _():
        tail_ref[...] = jnp.zeros_like(tail_ref)
        state_ref[...] = jnp.zeros_like(state_ref)

    x = qkv_ref[...]
    ext = jnp.concatenate([tail_ref[...], x], axis=0)
    cw = cw_ref[...]
    conv = cw[CONV_W - 1:CONV_W, :] * x
    for s in range(1, CONV_W):
        conv = conv + cw[CONV_W - 1 - s:CONV_W - s, :] * ext[SUBLANES - s:SUBLANES - s + C, :]
    tail_ref[...] = x[C - SUBLANES:, :]
    act = _silu(conv)

    ab = ab_ref[...]
    g = -jnp.exp(alog_ref[...]) * _softplus(ab + dtb_ref[...])
    beta_all = _sigmoid(ab)
    ri = _iota((C, C), 0)
    ci = _iota((C, C), 1)
    tril = (ri >= ci)
    strict = (ri > ci)
    gc = jnp.dot(tril.astype(F32), g, preferred_element_type=F32,
                 precision=lax.Precision.HIGHEST)
    gc_t = jnp.transpose(gc)
    eye = (ri == ci).astype(F32)
    nw = nw_ref[...]
    zz = z_ref[...]

    for h in range(GDN_HEADS):
        hs = slice(h * Dh, (h + 1) * Dh)
        q = act[:, h * Dh:(h + 1) * Dh]
        k = act[:, GDN_W + h * Dh:GDN_W + (h + 1) * Dh]
        v = act[:, 2 * GDN_W + h * Dh:2 * GDN_W + (h + 1) * Dh]
        q = q * lax.rsqrt(jnp.sum(q * q, axis=-1, keepdims=True) + 1e-6) * (Dh ** -0.5)
        k = k * lax.rsqrt(jnp.sum(k * k, axis=-1, keepdims=True) + 1e-6)
        beta = beta_all[:, SUBLANES + h:SUBLANES + h + 1]
        gcol = gc[:, h:h + 1]
        grow = gc_t[h:h + 1, :]
        glast = gc[C - 1:C, h:h + 1]
        decay = jnp.exp(jnp.where(tril, gcol - grow, -jnp.inf))
        eg = jnp.exp(gcol)
        kb = k * beta
        kk = lax.dot_general(kb, k, (((1,), (1,)), ((), ())), preferred_element_type=F32)
        xm = -jnp.where(strict, kk * decay, 0.0)
        tm = eye + xm
        p = xm
        for _ in range(5):
            p = jnp.dot(p, p, preferred_element_type=F32)
            tm = tm + jnp.dot(tm, p, preferred_element_type=F32)
        rhs = jnp.concatenate([v * beta, kb * eg], axis=1)
        sol = jnp.dot(tm, rhs, preferred_element_type=F32)
        u = sol[:, :Dh]
        w = sol[:, Dh:]
        qk = lax.dot_general(q, k, (((1,), (1,)), ((), ())), preferred_element_type=F32)
        qk = jnp.where(tril, qk * decay, 0.0)
        q_dec = q * eg
        k_dec = k * jnp.exp(glast - gcol)
        state = state_ref[h]
        ws = jnp.dot(jnp.concatenate([w, q_dec], axis=0), state, preferred_element_type=F32)
        v_new = u - ws[:C]
        o = ws[C:] + jnp.dot(qk, v_new, preferred_element_type=F32)
        state_ref[h] = state * jnp.exp(glast) + lax.dot_general(
            k_dec, v_new, (((0,), (0,)), ((), ())), preferred_element_type=F32)
        o = o * lax.rsqrt(jnp.mean(o * o, axis=-1, keepdims=True) + 1e-6)
        o_ref[:, hs] = (o * nw * _silu(zz[:, hs])).astype(o_ref.dtype)


def _gdn(proj_g, tail, conv_w, a_log, dt_bias, norm_w):
    T = proj_g.shape[0]
    C = GDN_CHUNK
    pad = lambda t: jnp.zeros((1, LANES), F32).at[0, :GDN_HEADS].set(t)
    ab_block = tail.shape[1] // LANES - 1
    return pl.pallas_call(
        _gdn_kernel,
        grid=(T // C,),
        in_specs=[pl.BlockSpec((C, 3 * GDN_W), lambda i: (i, 0)),
                  pl.BlockSpec((C, GDN_W), lambda i: (i, 3)),
                  pl.BlockSpec((C, LANES), lambda i: (i, ab_block)),
                  pl.BlockSpec((CONV_W, 3 * GDN_W), lambda i: (0, 0)),
                  pl.BlockSpec((1, LANES), lambda i: (0, 0)),
                  pl.BlockSpec((1, LANES), lambda i: (0, 0)),
                  pl.BlockSpec((1, HEAD_DIM), lambda i: (0, 0))],
        out_specs=pl.BlockSpec((C, GDN_W), lambda i: (i, 0)),
        out_shape=jax.ShapeDtypeStruct((T, GDN_W), BF16),
        scratch_shapes=[pltpu.VMEM((SUBLANES, 3 * GDN_W), F32),
                        pltpu.VMEM((GDN_HEADS, HEAD_DIM, HEAD_DIM), F32)],
        compiler_params=_cparams(("arbitrary",)),
        name="gdn",
    )(proj_g, proj_g, tail, conv_w, pad(a_log), pad(dt_bias), norm_w.reshape(1, HEAD_DIM))


def _gelu(x):
    return 0.5 * x * (1.0 + lax.erf(x * (2.0 ** -0.5)))


def _sg_kernel(u_ref, v_ref, nw_ref, nb_ref, w_ref, b_ref, o_ref):
    C = SG_CHUNK
    causal = _iota((C, C), 0) >= _iota((C, C), 1)
    for g in range(SG_GROUPS):
        gs = slice(g * HEAD_DIM, (g + 1) * HEAD_DIM)
        u = _gelu(u_ref[:, gs])
        v = _gelu(v_ref[:, gs])
        mu = jnp.mean(v, axis=-1, keepdims=True)
        var = jnp.mean(jnp.square(v - mu), axis=-1, keepdims=True)
        vn = (v - mu) * lax.rsqrt(var + LN_EPS) * nw_ref[:, gs] + nb_ref[:, gs]
        ws = jnp.where(causal, w_ref[g], 0.0)
        mixed = jnp.dot(ws, vn, preferred_element_type=F32) + b_ref[:, g:g + 1]
        o_ref[:, gs] = (u * mixed).astype(o_ref.dtype)


def _spatial_gating(tail, norm_w, norm_b, sg_w, sg_b):
    T = tail.shape[0]
    C = SG_CHUNK
    return pl.pallas_call(
        _sg_kernel,
        grid=(T // C,),
        in_specs=[pl.BlockSpec((C, SG_W), lambda i: (i, 0)),
                  pl.BlockSpec((C, SG_W), lambda i: (i, 1)),
                  pl.BlockSpec((1, SG_W), lambda i: (0, 0)),
                  pl.BlockSpec((1, SG_W), lambda i: (0, 0)),
                  pl.BlockSpec((SG_GROUPS, C, C), lambda i: (0, 0, 0)),
                  pl.BlockSpec((C, SG_GROUPS), lambda i: (0, 0))],
        out_specs=pl.BlockSpec((C, SG_W), lambda i: (i, 0)),
        out_shape=jax.ShapeDtypeStruct((T, SG_W), BF16),
        compiler_params=_cparams(("arbitrary",)),
        name="spatial_gating",
    )(tail, tail, norm_w.reshape(1, SG_W), norm_b.reshape(1, SG_W), sg_w, sg_b.T)


def _layer_norm_rows(r, w, b):
    mu = jnp.mean(r, axis=-1, keepdims=True)
    d = r - mu
    var = jnp.mean(d * d, axis=-1, keepdims=True)
    return d * lax.rsqrt(var + LN_EPS) * w + b


def _outproj_kernel(x_ref, ysb_ref, ygdn_ref, ysg_ref, w_ref, gt_ref, lw_ref, lb_ref,
                    sc_ref, sh_ref, xo_ref, ho_ref, *, alpha):
    y = jnp.dot(ysb_ref[...], w_ref[0:SB_W, :], preferred_element_type=F32)
    y = y + jnp.dot(ygdn_ref[...], w_ref[SB_W:SB_W + GDN_W, :], preferred_element_type=F32)
    y = y + jnp.dot(ysg_ref[...], w_ref[SB_W + GDN_W:, :], preferred_element_type=F32)
    r = alpha * x_ref[...] + (1.0 + gt_ref[...]) * y
    xn = _layer_norm_rows(r, lw_ref[...], lb_ref[...])
    xo_ref[...] = xn
    ho_ref[...] = xn * (1.0 + sc_ref[...]) + sh_ref[...]


def _outproj_ln(x, y_sb, y_gdn, y_sg, w_out_bf16, mod, ln_w, ln_b, alpha):
    T, D = x.shape
    tm = 256
    row = lambda idx: pl.BlockSpec((1, D), lambda i: (0, idx))
    return pl.pallas_call(
        functools.partial(_outproj_kernel, alpha=alpha),
        grid=(T // tm,),
        in_specs=[pl.BlockSpec((tm, D), lambda i: (i, 0)),
                  pl.BlockSpec((tm, SB_W), lambda i: (i, 0)),
                  pl.BlockSpec((tm, GDN_W), lambda i: (i, 0)),
                  pl.BlockSpec((tm, SG_W), lambda i: (i, 0)),
                  pl.BlockSpec((D, D), lambda i: (0, 0)),
                  row(2), row(0), row(0), row(4), row(3)],
        out_specs=[pl.BlockSpec((tm, D), lambda i: (i, 0)),
                   pl.BlockSpec((tm, D), lambda i: (i, 0))],
        out_shape=[jax.ShapeDtypeStruct((T, D), F32), jax.ShapeDtypeStruct((T, D), F32)],
        compiler_params=_cparams(("arbitrary",)),
        name="outproj_ln",
    )(x, y_sb, y_gdn, y_sg, w_out_bf16, mod, ln_w.reshape(1, D), ln_b.reshape(1, D), mod, mod)


def _first_max(vals, idx, axis):
    m = jnp.max(vals, axis=axis, keepdims=True)
    big = jnp.iinfo(jnp.int32).max
    first = jnp.min(jnp.where(vals == m, idx, big), axis=axis, keepdims=True)
    return m, first


def _router_kernel(h_ref, wt_ref, b_ref, idx_ref, wgt_ref, rank_ref, cnt_ref, carry_ref):
    E, G, S = N_EXPERTS, N_GROUPS, GROUP_SIZE
    tm = h_ref.shape[0]
    step = pl.program_id(0)

    @pl.when(step == 0)
    def _():
        carry_ref[...] = jnp.zeros_like(carry_ref)

    h = h_ref[...]
    h_hi = h.astype(BF16)
    h_lo = (h - h_hi.astype(F32)).astype(BF16)
    wt = wt_ref[...]
    w_hi = wt.astype(BF16)
    w_lo = (wt - w_hi.astype(F32)).astype(BF16)
    nt = (((1,), (1,)), ((), ()))
    logits = (lax.dot_general(w_hi, h_hi, nt, preferred_element_type=F32)
              + lax.dot_general(w_hi, h_lo, nt, preferred_element_type=F32)
              + lax.dot_general(w_lo, h_hi, nt, preferred_element_type=F32))
    scores = _sigmoid(logits)
    sel = scores + b_ref[:, 0:1]

    sel3 = sel.reshape(G, S, tm)
    e_in_g = _iota((G, S, tm), 1)
    m1, i1 = _first_max(sel3, e_in_g, 1)
    m2 = jnp.max(jnp.where(e_in_g == i1, -jnp.inf, sel3), axis=1, keepdims=True)
    gs = m1 + m2

    g_idx = _iota((G, 1, tm), 0)
    gmask = jnp.zeros((G, 1, tm), jnp.bool_)
    for _ in range(TOPK_GROUPS):
        _, gi = _first_max(gs, g_idx, 0)
        hit = g_idx == gi
        gmask = jnp.logical_or(gmask, hit)
        gs = jnp.where(hit, -jnp.inf, gs)
    cur = jnp.where(gmask, sel3, -jnp.inf).reshape(E, tm)

    e_idx = _iota((E, tm), 0)
    chosen = jnp.zeros((E, tm), jnp.bool_)
    idx_rows = []
    score_rows = []
    for _ in range(TOP_K):
        _, ei = _first_max(cur, e_idx, 0)
        hit = e_idx == ei
        chosen = jnp.logical_or(chosen, hit)
        cur = jnp.where(hit, -jnp.inf, cur)
        idx_rows.append(ei)
        score_rows.append(jnp.sum(jnp.where(hit, scores, 0.0), axis=0, keepdims=True))
    top_idx = jnp.concatenate(idx_rows, axis=0)
    top_sc = jnp.concatenate(score_rows, axis=0)
    idx_ref[...] = top_idx
    wgt_ref[...] = top_sc / jnp.sum(top_sc, axis=0, keepdims=True) * ROUTED_SCALE

    chosen_b = chosen.astype(BF16)
    before = (_iota((tm, tm), 0) < _iota((tm, tm), 1)).astype(BF16)
    excl = jnp.dot(chosen_b, before, preferred_element_type=F32) + carry_ref[:, 0:1]
    rank_rows = [jnp.sum(jnp.where(e_idx == idx_rows[k], excl, 0.0), axis=0, keepdims=True)
                 for k in range(TOP_K)]
    rank_ref[...] = jnp.concatenate(rank_rows, axis=0).astype(I32)
    carry_ref[...] = carry_ref[...] + jnp.sum(chosen.astype(F32), axis=1, keepdims=True)
    cnt_ref[...] = carry_ref[...]


def _router(h2, w_router_t, router_bias):
    T, D = h2.shape
    tm = ROUTER_TM
    E, K = N_EXPERTS, TOP_K
    bias = jnp.broadcast_to(router_bias.reshape(E, 1), (E, LANES))
    tok = lambda: pl.BlockSpec((K, tm), lambda i: (0, i))
    return pl.pallas_call(
        _router_kernel,
        grid=(T // tm,),
        in_specs=[pl.BlockSpec((tm, D), lambda i: (i, 0)),
                  pl.BlockSpec((E, D), lambda i: (0, 0)),
                  pl.BlockSpec((E, LANES), lambda i: (0, 0))],
        out_specs=[tok(), tok(), tok(), pl.BlockSpec((E, LANES), lambda i: (0, 0))],
        out_shape=[jax.ShapeDtypeStruct((K, T), I32), jax.ShapeDtypeStruct((K, T), F32),
                   jax.ShapeDtypeStruct((K, T), I32), jax.ShapeDtypeStruct((E, LANES), F32)],
        scratch_shapes=[pltpu.VMEM((E, LANES), F32)],
        compiler_params=_cparams(("arbitrary",)),
        name="router",
    )(h2, w_router_t, bias)


def _plan_kernel(cnt_ref, idx_ref, rank_ref, pos_ref, texp_ref, end_ref, *, tile, n_tiles):
    E = N_EXPERTS
    tm = idx_ref.shape[1]
    cnt = cnt_ref[...]
    ntile = jnp.floor((cnt + (tile - 1)) * (1.0 / tile))
    lower = (_iota((E, E), 0) >= _iota((E, E), 1)).astype(F32)
    tile_end = jnp.dot(lower, ntile, preferred_element_type=F32)
    row_start = (tile_end - ntile) * tile
    e_idx = _iota((E, tm), 0)
    start_col = row_start[:, 0:1]
    for k in range(TOP_K):
        base = jnp.sum(jnp.where(e_idx == idx_ref[k:k + 1, :], start_col, 0.0), axis=0, keepdims=True)
        pos_ref[k:k + 1, :] = base.astype(I32) + rank_ref[k:k + 1, :]
    tile_i = _iota((E, n_tiles), 1).astype(F32)
    texp = jnp.sum((tile_end[:, 0:1] <= tile_i).astype(F32), axis=0, keepdims=True)
    texp_ref[...] = jnp.minimum(texp, E - 1.0).astype(I32)
    end_ref[...] = (tile_end * tile).astype(I32)


def _plan(cnt, idx_t, rank_t, tile, n_tiles):
    K, T = idx_t.shape
    E = N_EXPERTS
    tm = min(2048, T)
    tok = lambda: pl.BlockSpec((K, tm), lambda i: (0, i))
    return pl.pallas_call(
        functools.partial(_plan_kernel, tile=tile, n_tiles=n_tiles),
        grid=(T // tm,),
        in_specs=[pl.BlockSpec((E, LANES), lambda i: (0, 0)), tok(), tok()],
        out_specs=[tok(), pl.BlockSpec((1, n_tiles), lambda i: (0, 0)),
                   pl.BlockSpec((E, LANES), lambda i: (0, 0))],
        out_shape=[jax.ShapeDtypeStruct((K, T), I32), jax.ShapeDtypeStruct((1, n_tiles), I32),
                   jax.ShapeDtypeStruct((E, LANES), I32)],
        compiler_params=_cparams(("arbitrary",)),
        name="route_plan",
    )(cnt, idx_t, rank_t)


def _dispatch_kernel(pos_ref, end_ref, h_ref, xs_ref, zero_ref, sem, *, td, n_tok, tile):
    step = pl.program_id(0)

    @pl.when(step == 0)
    def _():
        zero_ref[...] = jnp.zeros_like(zero_ref)

        def fill(e, n):
            end = end_ref[e]
            prev = jnp.where(e > 0, end_ref[jnp.maximum(e - 1, 0)], 0)
            has = end > prev

            @pl.when(has)
            def _():
                start = pl.multiple_of(end - tile, tile)
                pltpu.make_async_copy(zero_ref, xs_ref.at[pl.ds(start, tile)], sem).start()

            return n + has.astype(I32)

        n = lax.fori_loop(0, N_EXPERTS, fill, jnp.int32(0))

        used = end_ref[N_EXPERTS - 1] // tile
        n_tiles = xs_ref.shape[0] // tile

        def fill_unused(i, c):
            start = pl.multiple_of(i * tile, tile)
            pltpu.make_async_copy(zero_ref, xs_ref.at[pl.ds(start, tile)], sem).start()
            return c

        lax.fori_loop(used, n_tiles, fill_unused, 0)
        n = n + (n_tiles - used)

        def drain(_, c):
            pltpu.make_async_copy(zero_ref, xs_ref.at[pl.ds(0, tile)], sem).wait()
            return c

        lax.fori_loop(0, n, drain, 0)

    def issue(t, c):
        tok = step * td + t
        for k in range(TOP_K):
            p = pos_ref[k * n_tok + tok]
            pltpu.make_async_copy(h_ref.at[pl.ds(tok, 1)], xs_ref.at[pl.ds(p, 1)], sem).start()
        return c

    lax.fori_loop(0, td, issue, 0)

    def drain_rows(_, c):
        pltpu.make_async_copy(h_ref.at[pl.ds(0, 1)], xs_ref.at[pl.ds(0, 1)], sem).wait()
        return c

    lax.fori_loop(0, td * TOP_K, drain_rows, 0)


def _dispatch(pos_flat, seg_end, h2, n_rows, tile):
    T, D = h2.shape
    td = DISPATCH_TD
    return pl.pallas_call(
        functools.partial(_dispatch_kernel, td=td, n_tok=T, tile=tile),
        grid_spec=pltpu.PrefetchScalarGridSpec(
            num_scalar_prefetch=2,
            grid=(T // td,),
            in_specs=[pl.BlockSpec(memory_space=pl.ANY)],
            out_specs=pl.BlockSpec(memory_space=pl.ANY),
            scratch_shapes=[pltpu.VMEM((tile, D), h2.dtype), pltpu.SemaphoreType.DMA(())]),
        out_shape=jax.ShapeDtypeStruct((n_rows, D), h2.dtype),
        compiler_params=_cparams(("arbitrary",)),
        name="moe_dispatch",
    )(pos_flat, seg_end, h2)


def _expert_kernel(texp_ref, used_ref, x_ref, w1_ref, w3_ref, w2_ref, o_ref):
    live = pl.program_id(0) < used_ref[0]

    @pl.when(live)
    def _():
        x = x_ref[...].astype(BF16)
        a = _silu(jnp.dot(x, w1_ref[0], preferred_element_type=F32))
        a = a * jnp.dot(x, w3_ref[0], preferred_element_type=F32)
        o_ref[...] = jnp.dot(a.astype(BF16), w2_ref[0], preferred_element_type=F32)

    @pl.when(jnp.logical_not(live))
    def _():
        o_ref[...] = jnp.zeros_like(o_ref)


def _experts(tile_expert, tiles_used, xs, w1, w3, w2, layer):
    R, D = xs.shape
    F = w1.shape[-1]
    tm = EXPERT_TM
    n_tiles = R // tm
    row = lambda i, te, nu: (jnp.minimum(i, nu[0] - 1), 0)
    return pl.pallas_call(
        _expert_kernel,
        grid_spec=pltpu.PrefetchScalarGridSpec(
            num_scalar_prefetch=2,
            grid=(n_tiles,),
            in_specs=[pl.BlockSpec((tm, D), row),
                      pl.BlockSpec((None, 1, D, F), lambda i, te, nu: (layer, te[i], 0, 0)),
                      pl.BlockSpec((None, 1, D, F), lambda i, te, nu: (layer, te[i], 0, 0)),
                      pl.BlockSpec((None, 1, F, D), lambda i, te, nu: (layer, te[i], 0, 0))],
            out_specs=pl.BlockSpec((tm, D), lambda i, te, nu: (i, 0))),
        out_shape=jax.ShapeDtypeStruct((R, D), F32),
        compiler_params=_cparams(("arbitrary",)),
        name="moe_experts",
    )(tile_expert, tiles_used, xs, w1, w3, w2)


def _combine_kernel(pos_ref, os_ref, x_ref, h_ref, g_ref, s1_ref, s3_ref, s2_ref, gt_ref,
                    lw_ref, lb_ref, sc_ref, sh_ref, xo_ref, ho_ref, buf_ref, sem,
                    *, tc, n_tok, alpha):
    step = pl.program_id(0)

    def issue(t, c):
        tok = step * tc + t
        for k in range(TOP_K):
            p = pos_ref[k * n_tok + tok]
            pltpu.make_async_copy(os_ref.at[pl.ds(p, 1)], buf_ref.at[k, pl.ds(t, 1)], sem).start()
        return c

    lax.fori_loop(0, tc, issue, 0)

    hb = h_ref[...].astype(BF16)
    a = _silu(jnp.dot(hb, s1_ref[...], preferred_element_type=F32))
    a = a * jnp.dot(hb, s3_ref[...], preferred_element_type=F32)
    y = jnp.dot(a.astype(BF16), s2_ref[...], preferred_element_type=F32)

    def drain(_, c):
        pltpu.make_async_copy(os_ref.at[pl.ds(0, 1)], buf_ref.at[0, pl.ds(0, 1)], sem).wait()
        return c

    lax.fori_loop(0, tc * TOP_K, drain, 0)

    g = g_ref[...]
    for k in range(TOP_K):
        y = y + g[:, k:k + 1] * buf_ref[k]
    r = alpha * x_ref[...] + (1.0 + gt_ref[...]) * y
    xn = _layer_norm_rows(r, lw_ref[...], lb_ref[...])
    xo_ref[...] = xn
    ho_ref[...] = (xn * (1.0 + sc_ref[...]) + sh_ref[...]).astype(ho_ref.dtype)


def _combine(pos_flat, os_rows, x, h2, gates, s1, s3, s2, mod, ln_w, ln_b, mod_next, alpha):
    T, D = x.shape
    F = s1.shape[-1]
    tc = COMBINE_TC
    row = lambda idx: pl.BlockSpec((1, D), lambda i, p: (0, idx))
    full = lambda a, b: pl.BlockSpec((a, b), lambda i, p: (0, 0))
    return pl.pallas_call(
        functools.partial(_combine_kernel, tc=tc, n_tok=T, alpha=alpha),
        grid_spec=pltpu.PrefetchScalarGridSpec(
            num_scalar_prefetch=1,
            grid=(T // tc,),
            in_specs=[pl.BlockSpec(memory_space=pl.ANY),
                      pl.BlockSpec((tc, D), lambda i, p: (i, 0)),
                      pl.BlockSpec((tc, D), lambda i, p: (i, 0)),
                      pl.BlockSpec((tc, TOP_K), lambda i, p: (i, 0)),
                      full(D, F), full(D, F), full(F, D),
                      row(5), row(0), row(0), row(1), row(0)],
            out_specs=[pl.BlockSpec((tc, D), lambda i, p: (i, 0)),
                       pl.BlockSpec((tc, D), lambda i, p: (i, 0))],
            scratch_shapes=[pltpu.VMEM((TOP_K, tc, D), F32), pltpu.SemaphoreType.DMA(())]),
        out_shape=[jax.ShapeDtypeStruct((T, D), F32), jax.ShapeDtypeStruct((T, D), BF16)],
        compiler_params=_cparams(("arbitrary",)),
        name="moe_combine",
    )(pos_flat, os_rows, x, h2, gates, s1, s3, s2, mod, ln_w.reshape(1, D), ln_b.reshape(1, D),
      mod_next, mod_next)


def kernel(x, c, w_ada, b_ada, w_in, conv_w, gdn_a_log, gdn_dt_bias, gdn_norm_w, sg_norm_w, sg_norm_b, sg_w, sg_b, w_out, ln1_w, ln1_b, w_router, router_bias, exp_w1, exp_w3, exp_w2, sh_w1, sh_w3, sh_w2, ln2_w, ln2_b):
    B, T, D = x.shape
    assert B == 1
    L = w_ada.shape[0]
    alpha = (2.0 * L) ** 0.25
    main_cols = 3 * SB_W + 4 * GDN_W
    assert main_cols % 512 == 0 and w_in.shape[2] == main_cols + 2 * GDN_HEADS + 2 * SG_W
    n_tiles = (T * TOP_K) // EXPERT_TM + N_EXPERTS
    n_rows = n_tiles * EXPERT_TM
    n_tiles_pad = -(-n_tiles // LANES) * LANES

    xt = x.reshape(T, D)
    mod_all = _ada_mod(c, w_ada, b_ada)

    ab_cols = w_in[:, :, main_cols:main_cols + 2 * GDN_HEADS]
    ab_cols = jnp.pad(ab_cols, ((0, 0), (0, 0), (0, LANES - 2 * GDN_HEADS)))
    w_tail = jnp.concatenate([w_in[:, :, main_cols + 2 * GDN_HEADS:], ab_cols], axis=-1)
    w_out_b = w_out.astype(BF16)
    w_router_t = jnp.swapaxes(w_router, 1, 2)
    e1, e3, e2 = exp_w1.astype(BF16), exp_w3.astype(BF16), exp_w2.astype(BF16)
    s1, s3, s2 = sh_w1.astype(BF16), sh_w3.astype(BF16), sh_w2.astype(BF16)

    h = _modulate(xt, mod_all[0], 1, 0)
    for l in range(L):
        mod = mod_all[l]
        mod_next = mod_all[min(l + 1, L - 1)]
        sb_qkv = _proj(h, w_in, l, 0, 3 * SB_W // 512, 512, BF16)
        proj_g = _proj(h, w_in, l, 3 * SB_W // 512, 4 * GDN_W // 512, 512, F32)
        tail = _proj(h, w_tail, l, 0, 1, w_tail.shape[2], F32)
        y_sb = _sb_attention(sb_qkv)
        y_gdn = _gdn(proj_g, tail, conv_w[l], gdn_a_log[l], gdn_dt_bias[l], gdn_norm_w[l])
        y_sg = _spatial_gating(tail, sg_norm_w[l], sg_norm_b[l], sg_w[l], sg_b[l])
        xt, h2 = _outproj_ln(xt, y_sb, y_gdn, y_sg, w_out_b[l], mod, ln1_w[l], ln1_b[l], alpha)
        idx_t, wgt_t, rank_t, cnt = _router(h2, w_router_t[l], router_bias[l])
        pos_t, tile_expert, seg_end = _plan(cnt, idx_t, rank_t, EXPERT_TM, n_tiles_pad)
        pos_flat = pos_t.reshape(-1)
        seg_end = seg_end[:, 0]
        tiles_used = seg_end[N_EXPERTS - 1:] // EXPERT_TM
        xs = _dispatch(pos_flat, seg_end, h2, n_rows, EXPERT_TM)
        os_rows = _experts(tile_expert.reshape(-1), tiles_used, xs, e1, e3, e2, l)
        xt, h = _combine(pos_flat, os_rows, xt, h2, wgt_t.T, s1[l], s3[l], s2[l], mod,
                         ln2_w[l], ln2_b[l], mod_next, alpha)
    return xt.reshape(B, T, D)
```

```python
import functools

import jax
import jax.numpy as jnp
from jax import lax
from jax.experimental import pallas as pl
from jax.experimental.pallas import tpu as pltpu

F32 = jnp.float32
BF16 = jnp.bfloat16
I32 = jnp.int32

LANES = 128
SUBLANES = 8
VMEM_LIMIT = 56 * 1024 * 1024

HEAD_DIM = 128
SB_HEADS = 4
GDN_HEADS = 8
SG_GROUPS = 4
SB_W = SB_HEADS * HEAD_DIM
GDN_W = GDN_HEADS * HEAD_DIM
SG_W = SG_GROUPS * HEAD_DIM
GDN_CHUNK = 64
CONV_W = 4
SG_CHUNK = 128
N_EXPERTS = 64
TOP_K = 8
N_GROUPS = 8
GROUP_SIZE = N_EXPERTS // N_GROUPS
TOPK_GROUPS = 4
ROUTED_SCALE = 2.5
LN_EPS = 1e-5

SB_TQ = 256
SB_TK = 256
SB_UNDERFLOW = 110.0
EXPERT_TM = 256
ROUTER_TM = 512
DISPATCH_TD = 256
COMBINE_TC = 128


def _cparams(sem):
    return pltpu.CompilerParams(dimension_semantics=sem, vmem_limit_bytes=VMEM_LIMIT)


def _sigmoid(x):
    return 1.0 / (1.0 + jnp.exp(-x))


def _silu(x):
    return x * _sigmoid(x)


def _softplus(x):
    return jnp.maximum(x, 0.0) + jnp.log(1.0 + jnp.exp(-jnp.abs(x)))


def _iota(shape, dim):
    return lax.broadcasted_iota(I32, shape, dim)


def _ada_kernel(c_ref, w_ref, b_ref, o_ref):
    cb = _silu(c_ref[...])
    tn = w_ref.shape[2]
    for j in range(tn // LANES):
        sl = slice(j * LANES, (j + 1) * LANES)
        o_ref[0, :, sl] = jnp.sum(w_ref[0, :, sl] * cb, axis=0, keepdims=True) + b_ref[0, :, sl]


def _ada_mod(c, w_ada, b_ada):
    L, D, N = w_ada.shape
    tn = 1536
    cb = jnp.broadcast_to(c.reshape(D, 1), (D, LANES))
    return pl.pallas_call(
        _ada_kernel,
        grid=(L, N // tn),
        in_specs=[pl.BlockSpec((D, LANES), lambda l, j: (0, 0)),
                  pl.BlockSpec((1, D, tn), lambda l, j: (l, 0, j)),
                  pl.BlockSpec((1, 1, tn), lambda l, j: (l, 0, j))],
        out_specs=pl.BlockSpec((1, 1, tn), lambda l, j: (l, 0, j)),
        out_shape=jax.ShapeDtypeStruct((L, 1, N), F32),
        compiler_params=_cparams(("arbitrary", "arbitrary")),
        name="ada_mod",
    )(cb, w_ada, b_ada.reshape(L, 1, N))


def _modulate_kernel(x_ref, sc_ref, sh_ref, o_ref):
    o_ref[...] = (x_ref[...] * (1.0 + sc_ref[...]) + sh_ref[...]).astype(o_ref.dtype)


def _modulate(x, mod, sc_idx, sh_idx):
    T, D = x.shape
    tm = 512
    return pl.pallas_call(
        _modulate_kernel,
        grid=(T // tm,),
        in_specs=[pl.BlockSpec((tm, D), lambda i: (i, 0)),
                  pl.BlockSpec((1, D), lambda i: (0, sc_idx)),
                  pl.BlockSpec((1, D), lambda i: (0, sh_idx))],
        out_specs=pl.BlockSpec((tm, D), lambda i: (i, 0)),
        out_shape=jax.ShapeDtypeStruct((T, D), BF16),
        compiler_params=_cparams(("arbitrary",)),
        name="modulate",
    )(x, mod, mod)


def _proj_kernel(x_ref, w_ref, o_ref, wb_ref):
    @pl.when(pl.program_id(1) == 0)
    def _():
        wb_ref[...] = w_ref[0].astype(BF16)

    o_ref[...] = jnp.dot(x_ref[...], wb_ref[...], preferred_element_type=F32).astype(o_ref.dtype)


def _proj(h, w, layer, col_block0, n_blocks, tn, out_dtype):
    T, K = h.shape
    tm = min(1024, T)
    return pl.pallas_call(
        _proj_kernel,
        grid=(n_blocks, T // tm),
        in_specs=[pl.BlockSpec((tm, K), lambda j, i: (i, 0)),
                  pl.BlockSpec((1, K, tn), lambda j, i: (layer, 0, j + col_block0))],
        out_specs=pl.BlockSpec((tm, tn), lambda j, i: (i, j)),
        out_shape=jax.ShapeDtypeStruct((T, n_blocks * tn), out_dtype),
        scratch_shapes=[pltpu.VMEM((K, tn), BF16)],
        compiler_params=_cparams(("arbitrary", "arbitrary")),
        name="in_proj",
    )(h, w)


def _sb_kernel(q_ref, k_ref, v_ref, o_ref, *, tq, tk, scale):
    i = pl.program_id(1)
    q = q_ref[...]
    tri = (_iota((tk, tk), 0) >= _iota((tk, tk), 1)).astype(BF16)

    def block(j, r, diagonal):
        ks = pl.multiple_of(j * tk, tk)
        k = k_ref[pl.ds(ks, tk), :]
        v = v_ref[pl.ds(ks, tk), :]
        z = lax.dot_general(q, k, (((1,), (1,)), ((), ())), preferred_element_type=F32) * scale
        sp = _softplus(z)
        if diagonal:
            causal = _iota((tq, tk), 1) < _iota((tq, tk), 0)
            sp = jnp.where(causal, sp, 0.0)
        sp_hi = sp.astype(BF16)
        sp_lo = (sp - sp_hi.astype(F32)).astype(BF16)
        c = (jnp.dot(sp_hi, tri, preferred_element_type=F32)
             + jnp.dot(sp_lo, tri, preferred_element_type=F32))
        a = jnp.exp(z - c - r)
        if diagonal:
            a = jnp.where(causal, a, 0.0)
        pv = jnp.dot(a.astype(BF16), v, preferred_element_type=F32)
        return pv, r + c[:, 0:1]

    acc, r = block(i, jnp.zeros((tq, 1), F32), True)

    def cond(carry):
        step, _, _, rmin = carry
        return jnp.logical_and(step <= i, rmin < SB_UNDERFLOW)

    def body(carry):
        step, acc, r, _ = carry
        pv, r = block(i - step, r, False)
        return step + 1, acc + pv, r, jnp.min(r)

    _, acc, _, _ = lax.while_loop(cond, body, (jnp.int32(1), acc, r, jnp.min(r)))
    o_ref[...] = acc.astype(o_ref.dtype)


def _sb_attention(qkv):
    T = qkv.shape[0]
    tq, tk = SB_TQ, SB_TK
    assert tq == tk
    kern = functools.partial(_sb_kernel, tq=tq, tk=tk, scale=HEAD_DIM ** -0.5)
    return pl.pallas_call(
        kern,
        grid=(SB_HEADS, T // tq),
        in_specs=[pl.BlockSpec((tq, HEAD_DIM), lambda h, i: (i, h)),
                  pl.BlockSpec((T, HEAD_DIM), lambda h, i: (0, SB_HEADS + h)),
                  pl.BlockSpec((T, HEAD_DIM), lambda h, i: (0, 2 * SB_HEADS + h))],
        out_specs=pl.BlockSpec((tq, HEAD_DIM), lambda h, i: (i, h)),
        out_shape=jax.ShapeDtypeStruct((T, SB_W), BF16),
        compiler_params=_cparams(("arbitrary", "arbitrary")),
        name="sb_attention",
    )(qkv, qkv, qkv)


def _gdn_kernel(qkv_ref, z_ref, ab_ref, cw_ref, alog_ref, dtb_ref, nw_ref, o_ref,
                tail_ref, state_ref):
    C = GDN_CHUNK
    Dh = HEAD_DIM
    step = pl.program_id(0)

    @pl.when(step == 0)
    def _():
        tail_ref[...] = jnp.zeros_like(tail_ref)
        state_ref[...] = jnp.zeros_like(state_ref)

    x = qkv_ref[...]
    ext = jnp.concatenate([tail_ref[...], x], axis=0)
    cw = cw_ref[...]
    conv = cw[CONV_W - 1:CONV_W, :] * x
    for s in range(1, CONV_W):
        conv = conv + cw[CONV_W - 1 - s:CONV_W - s, :] * ext[SUBLANES - s:SUBLANES - s + C, :]
    tail_ref[...] = x[C - SUBLANES:, :]
    act = _silu(conv)

    ab = ab_ref[...]
    g = -jnp.exp(alog_ref[...]) * _softplus(ab + dtb_ref[...])
    beta_all = _sigmoid(ab)
    ri = _iota((C, C), 0)
    ci = _iota((C, C), 1)
    tril = (ri >= ci)
    strict = (ri > ci)
    gc = jnp.dot(tril.astype(F32), g, preferred_element_type=F32,
                 precision=lax.Precision.HIGHEST)
    gc_t = jnp.transpose(gc)
    eye = (ri == ci).astype(F32)
    nw = nw_ref[...]
    zz = z_ref[...]

    H = range(GDN_HEADS)
    nt = (((1,), (1,)), ((), ()))
    mm = lambda a, b: jnp.dot(a, b, preferred_element_type=F32)
    qs = [act[:, h * Dh:(h + 1) * Dh] for h in H]
    ks = [act[:, GDN_W + h * Dh:GDN_W + (h + 1) * Dh] for h in H]
    vs = [act[:, 2 * GDN_W + h * Dh:2 * GDN_W + (h + 1) * Dh] for h in H]
    qs = [q * lax.rsqrt(jnp.sum(q * q, axis=-1, keepdims=True) + 1e-6) * (Dh ** -0.5) for q in qs]
    ks = [k * lax.rsqrt(jnp.sum(k * k, axis=-1, keepdims=True) + 1e-6) for k in ks]
    beta = [beta_all[:, SUBLANES + h:SUBLANES + h + 1] for h in H]
    gcol = [gc[:, h:h + 1] for h in H]
    grow = [gc_t[h:h + 1, :] for h in H]
    glast = [gc[C - 1:C, h:h + 1] for h in H]
    decay = [jnp.exp(jnp.where(tril, gcol[h] - grow[h], -jnp.inf)) for h in H]
    eg = [jnp.exp(gcol[h]) for h in H]
    kb = [ks[h] * beta[h] for h in H]
    kk = [lax.dot_general(kb[h], ks[h], nt, preferred_element_type=F32) for h in H]
    xm = [-jnp.where(strict, kk[h] * decay[h], 0.0) for h in H]
    tm = [eye + xm[h] for h in H]
    p = xm
    for _ in range(5):
        p = [mm(p[h], p[h]) for h in H]
        tm = [tm[h] + mm(tm[h], p[h]) for h in H]
    sol = [mm(tm[h], jnp.concatenate([vs[h] * beta[h], kb[h] * eg[h]], axis=1)) for h in H]
    qk = [lax.dot_general(qs[h], ks[h], nt, preferred_element_type=F32) for h in H]
    qk = [jnp.where(tril, qk[h] * decay[h], 0.0) for h in H]
    k_dec = [ks[h] * jnp.exp(glast[h] - gcol[h]) for h in H]
    state = [state_ref[h] for h in H]
    ws = [mm(jnp.concatenate([sol[h][:, Dh:], qs[h] * eg[h]], axis=0), state[h]) for h in H]
    v_new = [sol[h][:, :Dh] - ws[h][:C] for h in H]
    o = [ws[h][C:] + mm(qk[h], v_new[h]) for h in H]
    for h in H:
        state_ref[h] = state[h] * jnp.exp(glast[h]) + lax.dot_general(
            k_dec[h], v_new[h], (((0,), (0,)), ((), ())), preferred_element_type=F32)
    for h in H:
        hs = slice(h * Dh, (h + 1) * Dh)
        on = o[h] * lax.rsqrt(jnp.mean(o[h] * o[h], axis=-1, keepdims=True) + 1e-6)
        o_ref[:, hs] = (on * nw * _silu(zz[:, hs])).astype(o_ref.dtype)


def _gdn(proj_g, tail, conv_w, a_log, dt_bias, norm_w):
    T = proj_g.shape[0]
    C = GDN_CHUNK
    pad = lambda t: jnp.zeros((1, LANES), F32).at[0, :GDN_HEADS].set(t)
    ab_block = tail.shape[1] // LANES - 1
    return pl.pallas_call(
        _gdn_kernel,
        grid=(T // C,),
        in_specs=[pl.BlockSpec((C, 3 * GDN_W), lambda i: (i, 0)),
                  pl.BlockSpec((C, GDN_W), lambda i: (i, 3)),
                  pl.BlockSpec((C, LANES), lambda i: (i, ab_block)),
                  pl.BlockSpec((CONV_W, 3 * GDN_W), lambda i: (0, 0)),
                  pl.BlockSpec((1, LANES), lambda i: (0, 0)),
                  pl.BlockSpec((1, LANES), lambda i: (0, 0)),
                  pl.BlockSpec((1, HEAD_DIM), lambda i: (0, 0))],
        out_specs=pl.BlockSpec((C, GDN_W), lambda i: (i, 0)),
        out_shape=jax.ShapeDtypeStruct((T, GDN_W), BF16),
        scratch_shapes=[pltpu.VMEM((SUBLANES, 3 * GDN_W), F32),
                        pltpu.VMEM((GDN_HEADS, HEAD_DIM, HEAD_DIM), F32)],
        compiler_params=_cparams(("arbitrary",)),
        name="gdn",
    )(proj_g, proj_g, tail, conv_w, pad(a_log), pad(dt_bias), norm_w.reshape(1, HEAD_DIM))


def _gelu(x):
    return 0.5 * x * (1.0 + lax.erf(x * (2.0 ** -0.5)))


def _sg_kernel(u_ref, v_ref, nw_ref, nb_ref, w_ref, b_ref, o_ref):
    C = SG_CHUNK
    causal = _iota((C, C), 0) >= _iota((C, C), 1)
    for g in range(SG_GROUPS):
        gs = slice(g * HEAD_DIM, (g + 1) * HEAD_DIM)
        u = _gelu(u_ref[:, gs])
        v = _gelu(v_ref[:, gs])
        mu = jnp.mean(v, axis=-1, keepdims=True)
        var = jnp.mean(jnp.square(v - mu), axis=-1, keepdims=True)
        vn = (v - mu) * lax.rsqrt(var + LN_EPS) * nw_ref[:, gs] + nb_ref[:, gs]
        ws = jnp.where(causal, w_ref[g], 0.0)
        mixed = jnp.dot(ws, vn, preferred_element_type=F32) + b_ref[:, g:g + 1]
        o_ref[:, gs] = (u * mixed).astype(o_ref.dtype)


def _spatial_gating(tail, norm_w, norm_b, sg_w, sg_b):
    T = tail.shape[0]
    C = SG_CHUNK
    return pl.pallas_call(
        _sg_kernel,
        grid=(T // C,),
        in_specs=[pl.BlockSpec((C, SG_W), lambda i: (i, 0)),
                  pl.BlockSpec((C, SG_W), lambda i: (i, 1)),
                  pl.BlockSpec((1, SG_W), lambda i: (0, 0)),
                  pl.BlockSpec((1, SG_W), lambda i: (0, 0)),
                  pl.BlockSpec((SG_GROUPS, C, C), lambda i: (0, 0, 0)),
                  pl.BlockSpec((C, SG_GROUPS), lambda i: (0, 0))],
        out_specs=pl.BlockSpec((C, SG_W), lambda i: (i, 0)),
        out_shape=jax.ShapeDtypeStruct((T, SG_W), BF16),
        compiler_params=_cparams(("arbitrary",)),
        name="spatial_gating",
    )(tail, tail, norm_w.reshape(1, SG_W), norm_b.reshape(1, SG_W), sg_w, sg_b.T)


def _layer_norm_rows(r, w, b):
    mu = jnp.mean(r, axis=-1, keepdims=True)
    d = r - mu
    var = jnp.mean(d * d, axis=-1, keepdims=True)
    return d * lax.rsqrt(var + LN_EPS) * w + b


def _outproj_kernel(x_ref, ysb_ref, ygdn_ref, ysg_ref, w_ref, gt_ref, lw_ref, lb_ref,
                    sc_ref, sh_ref, xo_ref, ho_ref, *, alpha):
    y = jnp.dot(ysb_ref[...], w_ref[0:SB_W, :], preferred_element_type=F32)
    y = y + jnp.dot(ygdn_ref[...], w_ref[SB_W:SB_W + GDN_W, :], preferred_element_type=F32)
    y = y + jnp.dot(ysg_ref[...], w_ref[SB_W + GDN_W:, :], preferred_element_type=F32)
    r = alpha * x_ref[...] + (1.0 + gt_ref[...]) * y
    xn = _layer_norm_rows(r, lw_ref[...], lb_ref[...])
    xo_ref[...] = xn
    ho_ref[...] = xn * (1.0 + sc_ref[...]) + sh_ref[...]


def _outproj_ln(x, y_sb, y_gdn, y_sg, w_out_bf16, mod, ln_w, ln_b, alpha):
    T, D = x.shape
    tm = 256
    row = lambda idx: pl.BlockSpec((1, D), lambda i: (0, idx))
    return pl.pallas_call(
        functools.partial(_outproj_kernel, alpha=alpha),
        grid=(T // tm,),
        in_specs=[pl.BlockSpec((tm, D), lambda i: (i, 0)),
                  pl.BlockSpec((tm, SB_W), lambda i: (i, 0)),
                  pl.BlockSpec((tm, GDN_W), lambda i: (i, 0)),
                  pl.BlockSpec((tm, SG_W), lambda i: (i, 0)),
                  pl.BlockSpec((D, D), lambda i: (0, 0)),
                  row(2), row(0), row(0), row(4), row(3)],
        out_specs=[pl.BlockSpec((tm, D), lambda i: (i, 0)),
                   pl.BlockSpec((tm, D), lambda i: (i, 0))],
        out_shape=[jax.ShapeDtypeStruct((T, D), F32), jax.ShapeDtypeStruct((T, D), F32)],
        compiler_params=_cparams(("arbitrary",)),
        name="outproj_ln",
    )(x, y_sb, y_gdn, y_sg, w_out_bf16, mod, ln_w.reshape(1, D), ln_b.reshape(1, D), mod, mod)


def _first_max(vals, idx, axis):
    m = jnp.max(vals, axis=axis, keepdims=True)
    big = jnp.iinfo(jnp.int32).max
    first = jnp.min(jnp.where(vals == m, idx, big), axis=axis, keepdims=True)
    return m, first


def _router_kernel(h_ref, wt_ref, b_ref, idx_ref, wgt_ref, rank_ref, cnt_ref, carry_ref):
    E, G, S = N_EXPERTS, N_GROUPS, GROUP_SIZE
    tm = h_ref.shape[0]
    step = pl.program_id(0)

    @pl.when(step == 0)
    def _():
        carry_ref[...] = jnp.zeros_like(carry_ref)

    h = h_ref[...]
    h_hi = h.astype(BF16)
    h_lo = (h - h_hi.astype(F32)).astype(BF16)
    wt = wt_ref[...]
    w_hi = wt.astype(BF16)
    w_lo = (wt - w_hi.astype(F32)).astype(BF16)
    nt = (((1,), (1,)), ((), ()))
    logits = (lax.dot_general(w_hi, h_hi, nt, preferred_element_type=F32)
              + lax.dot_general(w_hi, h_lo, nt, preferred_element_type=F32)
              + lax.dot_general(w_lo, h_hi, nt, preferred_element_type=F32))
    scores = _sigmoid(logits)
    sel = scores + b_ref[:, 0:1]

    sel3 = sel.reshape(G, S, tm)
    e_in_g = _iota((G, S, tm), 1)
    m1, i1 = _first_max(sel3, e_in_g, 1)
    m2 = jnp.max(jnp.where(e_in_g == i1, -jnp.inf, sel3), axis=1, keepdims=True)
    gs = m1 + m2

    g_idx = _iota((G, 1, tm), 0)
    gmask = jnp.zeros((G, 1, tm), jnp.bool_)
    for _ in range(TOPK_GROUPS):
        _, gi = _first_max(gs, g_idx, 0)
        hit = g_idx == gi
        gmask = jnp.logical_or(gmask, hit)
        gs = jnp.where(hit, -jnp.inf, gs)
    cur = jnp.where(gmask, sel3, -jnp.inf).reshape(E, tm)

    e_idx = _iota((E, tm), 0)
    chosen = jnp.zeros((E, tm), jnp.bool_)
    idx_rows = []
    score_rows = []
    for _ in range(TOP_K):
        _, ei = _first_max(cur, e_idx, 0)
        hit = e_idx == ei
        chosen = jnp.logical_or(chosen, hit)
        cur = jnp.where(hit, -jnp.inf, cur)
        idx_rows.append(ei)
        score_rows.append(jnp.sum(jnp.where(hit, scores, 0.0), axis=0, keepdims=True))
    top_idx = jnp.concatenate(idx_rows, axis=0)
    top_sc = jnp.concatenate(score_rows, axis=0)
    idx_ref[...] = top_idx
    wgt_ref[...] = top_sc / jnp.sum(top_sc, axis=0, keepdims=True) * ROUTED_SCALE

    chosen_b = chosen.astype(BF16)
    before = (_iota((tm, tm), 0) < _iota((tm, tm), 1)).astype(BF16)
    excl = jnp.dot(chosen_b, before, preferred_element_type=F32) + carry_ref[:, 0:1]
    rank_rows = [jnp.sum(jnp.where(e_idx == idx_rows[k], excl, 0.0), axis=0, keepdims=True)
                 for k in range(TOP_K)]
    rank_ref[...] = jnp.concatenate(rank_rows, axis=0).astype(I32)
    carry_ref[...] = carry_ref[...] + jnp.sum(chosen.astype(F32), axis=1, keepdims=True)
    cnt_ref[...] = carry_ref[...]


def _router(h2, w_router_t, router_bias):
    T, D = h2.shape
    tm = ROUTER_TM
    E, K = N_EXPERTS, TOP_K
    bias = jnp.broadcast_to(router_bias.reshape(E, 1), (E, LANES))
    tok = lambda: pl.BlockSpec((K, tm), lambda i: (0, i))
    return pl.pallas_call(
        _router_kernel,
        grid=(T // tm,),
        in_specs=[pl.BlockSpec((tm, D), lambda i: (i, 0)),
                  pl.BlockSpec((E, D), lambda i: (0, 0)),
                  pl.BlockSpec((E, LANES), lambda i: (0, 0))],
        out_specs=[tok(), tok(), tok(), pl.BlockSpec((E, LANES), lambda i: (0, 0))],
        out_shape=[jax.ShapeDtypeStruct((K, T), I32), jax.ShapeDtypeStruct((K, T), F32),
                   jax.ShapeDtypeStruct((K, T), I32), jax.ShapeDtypeStruct((E, LANES), F32)],
        scratch_shapes=[pltpu.VMEM((E, LANES), F32)],
        compiler_params=_cparams(("arbitrary",)),
        name="router",
    )(h2, w_router_t, bias)


def _plan_kernel(cnt_ref, idx_ref, rank_ref, pos_ref, texp_ref, end_ref, *, tile, n_tiles):
    E = N_EXPERTS
    tm = idx_ref.shape[1]
    cnt = cnt_ref[...]
    ntile = jnp.floor((cnt + (tile - 1)) * (1.0 / tile))
    lower = (_iota((E, E), 0) >= _iota((E, E), 1)).astype(F32)
    tile_end = jnp.dot(lower, ntile, preferred_element_type=F32)
    row_start = (tile_end - ntile) * tile
    e_idx = _iota((E, tm), 0)
    start_col = row_start[:, 0:1]
    for k in range(TOP_K):
        base = jnp.sum(jnp.where(e_idx == idx_ref[k:k + 1, :], start_col, 0.0), axis=0, keepdims=True)
        pos_ref[k:k + 1, :] = base.astype(I32) + rank_ref[k:k + 1, :]
    tile_i = _iota((E, n_tiles), 1).astype(F32)
    texp = jnp.sum((tile_end[:, 0:1] <= tile_i).astype(F32), axis=0, keepdims=True)
    texp_ref[...] = jnp.minimum(texp, E - 1.0).astype(I32)
    end_ref[...] = (tile_end * tile).astype(I32)


def _plan(cnt, idx_t, rank_t, tile, n_tiles):
    K, T = idx_t.shape
    E = N_EXPERTS
    tm = min(2048, T)
    tok = lambda: pl.BlockSpec((K, tm), lambda i: (0, i))
    return pl.pallas_call(
        functools.partial(_plan_kernel, tile=tile, n_tiles=n_tiles),
        grid=(T // tm,),
        in_specs=[pl.BlockSpec((E, LANES), lambda i: (0, 0)), tok(), tok()],
        out_specs=[tok(), pl.BlockSpec((1, n_tiles), lambda i: (0, 0)),
                   pl.BlockSpec((E, LANES), lambda i: (0, 0))],
        out_shape=[jax.ShapeDtypeStruct((K, T), I32), jax.ShapeDtypeStruct((1, n_tiles), I32),
                   jax.ShapeDtypeStruct((E, LANES), I32)],
        compiler_params=_cparams(("arbitrary",)),
        name="route_plan",
    )(cnt, idx_t, rank_t)


def _dispatch_kernel(pos_ref, end_ref, h_ref, xs_ref, zero_ref, sem, *, td, n_tok, tile):
    step = pl.program_id(0)

    @pl.when(step == 0)
    def _():
        zero_ref[...] = jnp.zeros_like(zero_ref)

        def fill(e, n):
            end = end_ref[e]
            prev = jnp.where(e > 0, end_ref[jnp.maximum(e - 1, 0)], 0)
            has = end > prev

            @pl.when(has)
            def _():
                start = pl.multiple_of(end - tile, tile)
                pltpu.make_async_copy(zero_ref, xs_ref.at[pl.ds(start, tile)], sem).start()

            return n + has.astype(I32)

        n = lax.fori_loop(0, N_EXPERTS, fill, jnp.int32(0))

        used = end_ref[N_EXPERTS - 1] // tile
        n_tiles = xs_ref.shape[0] // tile

        def fill_unused(i, c):
            start = pl.multiple_of(i * tile, tile)
            pltpu.make_async_copy(zero_ref, xs_ref.at[pl.ds(start, tile)], sem).start()
            return c

        lax.fori_loop(used, n_tiles, fill_unused, 0)
        n = n + (n_tiles - used)

        def drain(_, c):
            pltpu.make_async_copy(zero_ref, xs_ref.at[pl.ds(0, tile)], sem).wait()
            return c

        lax.fori_loop(0, n, drain, 0)

    def issue(t, c):
        tok = step * td + t
        for k in range(TOP_K):
            p = pos_ref[k * n_tok + tok]
            pltpu.make_async_copy(h_ref.at[pl.ds(t, 1)], xs_ref.at[pl.ds(p, 1)], sem).start()
        return c

    lax.fori_loop(0, td, issue, 0)

    def drain_rows(_, c):
        pltpu.make_async_copy(h_ref.at[pl.ds(0, 1)], xs_ref.at[pl.ds(0, 1)], sem).wait()
        return c

    lax.fori_loop(0, td * TOP_K, drain_rows, 0)


def _dispatch(pos_flat, seg_end, h2, n_rows, tile):
    T, D = h2.shape
    td = DISPATCH_TD
    return pl.pallas_call(
        functools.partial(_dispatch_kernel, td=td, n_tok=T, tile=tile),
        grid_spec=pltpu.PrefetchScalarGridSpec(
            num_scalar_prefetch=2,
            grid=(T // td,),
            in_specs=[pl.BlockSpec((td, D), lambda i, p, e: (i, 0))],
            out_specs=pl.BlockSpec(memory_space=pl.ANY),
            scratch_shapes=[pltpu.VMEM((tile, D), h2.dtype), pltpu.SemaphoreType.DMA(())]),
        out_shape=jax.ShapeDtypeStruct((n_rows, D), h2.dtype),
        compiler_params=_cparams(("arbitrary",)),
        name="moe_dispatch",
    )(pos_flat, seg_end, h2)


def _expert_kernel(texp_ref, used_ref, x_ref, w1_ref, w3_ref, w2_ref, o_ref):
    live = pl.program_id(0) < used_ref[0]

    @pl.when(live)
    def _():
        x = x_ref[...].astype(BF16)
        a = _silu(jnp.dot(x, w1_ref[0], preferred_element_type=F32))
        a = a * jnp.dot(x, w3_ref[0], preferred_element_type=F32)
        o_ref[...] = jnp.dot(a.astype(BF16), w2_ref[0], preferred_element_type=F32)

    @pl.when(jnp.logical_not(live))
    def _():
        o_ref[...] = jnp.zeros_like(o_ref)


def _experts(tile_expert, tiles_used, xs, w1, w3, w2, layer):
    R, D = xs.shape
    F = w1.shape[-1]
    tm = EXPERT_TM
    n_tiles = R // tm
    row = lambda i, te, nu: (jnp.minimum(i, nu[0] - 1), 0)
    return pl.pallas_call(
        _expert_kernel,
        grid_spec=pltpu.PrefetchScalarGridSpec(
            num_scalar_prefetch=2,
            grid=(n_tiles,),
            in_specs=[pl.BlockSpec((tm, D), row),
                      pl.BlockSpec((None, 1, D, F), lambda i, te, nu: (layer, te[i], 0, 0)),
                      pl.BlockSpec((None, 1, D, F), lambda i, te, nu: (layer, te[i], 0, 0)),
                      pl.BlockSpec((None, 1, F, D), lambda i, te, nu: (layer, te[i], 0, 0))],
            out_specs=pl.BlockSpec((tm, D), lambda i, te, nu: (i, 0))),
        out_shape=jax.ShapeDtypeStruct((R, D), F32),
        compiler_params=_cparams(("arbitrary",)),
        name="moe_experts",
    )(tile_expert, tiles_used, xs, w1, w3, w2)


def _combine_kernel(pos_ref, os_ref, x_ref, h_ref, g_ref, s1_ref, s3_ref, s2_ref, gt_ref,
                    lw_ref, lb_ref, sc_ref, sh_ref, xo_ref, ho_ref, buf_ref, sem,
                    *, tc, n_tok, alpha):
    step = pl.program_id(0)

    def issue(t, c):
        tok = step * tc + t
        for k in range(TOP_K):
            p = pos_ref[k * n_tok + tok]
            pltpu.make_async_copy(os_ref.at[pl.ds(p, 1)], buf_ref.at[k, pl.ds(t, 1)], sem).start()
        return c

    lax.fori_loop(0, tc, issue, 0)

    hb = h_ref[...].astype(BF16)
    a = _silu(jnp.dot(hb, s1_ref[...], preferred_element_type=F32))
    a = a * jnp.dot(hb, s3_ref[...], preferred_element_type=F32)
    y = jnp.dot(a.astype(BF16), s2_ref[...], preferred_element_type=F32)

    def drain(_, c):
        pltpu.make_async_copy(os_ref.at[pl.ds(0, 1)], buf_ref.at[0, pl.ds(0, 1)], sem).wait()
        return c

    lax.fori_loop(0, tc * TOP_K, drain, 0)

    g = g_ref[...]
    for k in range(TOP_K):
        y = y + g[:, k:k + 1] * buf_ref[k]
    r = alpha * x_ref[...] + (1.0 + gt_ref[...]) * y
    xn = _layer_norm_rows(r, lw_ref[...], lb_ref[...])
    xo_ref[...] = xn
    ho_ref[...] = (xn * (1.0 + sc_ref[...]) + sh_ref[...]).astype(ho_ref.dtype)


def _combine(pos_flat, os_rows, x, h2, gates, s1, s3, s2, mod, ln_w, ln_b, mod_next, alpha):
    T, D = x.shape
    F = s1.shape[-1]
    tc = COMBINE_TC
    row = lambda idx: pl.BlockSpec((1, D), lambda i, p: (0, idx))
    full = lambda a, b: pl.BlockSpec((a, b), lambda i, p: (0, 0))
    return pl.pallas_call(
        functools.partial(_combine_kernel, tc=tc, n_tok=T, alpha=alpha),
        grid_spec=pltpu.PrefetchScalarGridSpec(
            num_scalar_prefetch=1,
            grid=(T // tc,),
            in_specs=[pl.BlockSpec(memory_space=pl.ANY),
                      pl.BlockSpec((tc, D), lambda i, p: (i, 0)),
                      pl.BlockSpec((tc, D), lambda i, p: (i, 0)),
                      pl.BlockSpec((tc, TOP_K), lambda i, p: (i, 0)),
                      full(D, F), full(D, F), full(F, D),
                      row(5), row(0), row(0), row(1), row(0)],
            out_specs=[pl.BlockSpec((tc, D), lambda i, p: (i, 0)),
                       pl.BlockSpec((tc, D), lambda i, p: (i, 0))],
            scratch_shapes=[pltpu.VMEM((TOP_K, tc, D), F32), pltpu.SemaphoreType.DMA(())]),
        out_shape=[jax.ShapeDtypeStruct((T, D), F32), jax.ShapeDtypeStruct((T, D), BF16)],
        compiler_params=_cparams(("arbitrary",)),
        name="moe_combine",
    )(pos_flat, os_rows, x, h2, gates, s1, s3, s2, mod, ln_w.reshape(1, D), ln_b.reshape(1, D),
      mod_next, mod_next)


def kernel(x, c, w_ada, b_ada, w_in, conv_w, gdn_a_log, gdn_dt_bias, gdn_norm_w, sg_norm_w, sg_norm_b, sg_w, sg_b, w_out, ln1_w, ln1_b, w_router, router_bias, exp_w1, exp_w3, exp_w2, sh_w1, sh_w3, sh_w2, ln2_w, ln2_b):
    B, T, D = x.shape
    assert B == 1
    L = w_ada.shape[0]
    alpha = (2.0 * L) ** 0.25
    main_cols = 3 * SB_W + 4 * GDN_W
    assert main_cols % 512 == 0 and w_in.shape[2] == main_cols + 2 * GDN_HEADS + 2 * SG_W
    n_tiles = (T * TOP_K) // EXPERT_TM + N_EXPERTS
    n_rows = n_tiles * EXPERT_TM
    n_tiles_pad = -(-n_tiles // LANES) * LANES

    xt = x.reshape(T, D)
    mod_all = _ada_mod(c, w_ada, b_ada)

    ab_cols = w_in[:, :, main_cols:main_cols + 2 * GDN_HEADS]
    ab_cols = jnp.pad(ab_cols, ((0, 0), (0, 0), (0, LANES - 2 * GDN_HEADS)))
    w_tail = jnp.concatenate([w_in[:, :, main_cols + 2 * GDN_HEADS:], ab_cols], axis=-1)
    w_out_b = w_out.astype(BF16)
    w_router_t = jnp.swapaxes(w_router, 1, 2)
    e1, e3, e2 = exp_w1.astype(BF16), exp_w3.astype(BF16), exp_w2.astype(BF16)
    s1, s3, s2 = sh_w1.astype(BF16), sh_w3.astype(BF16), sh_w2.astype(BF16)

    h = _modulate(xt, mod_all[0], 1, 0)
    for l in range(L):
        mod = mod_all[l]
        mod_next = mod_all[min(l + 1, L - 1)]
        sb_qkv = _proj(h, w_in, l, 0, 3 * SB_W // 512, 512, BF16)
        proj_g = _proj(h, w_in, l, 3 * SB_W // 512, 4 * GDN_W // 512, 512, F32)
        tail = _proj(h, w_tail, l, 0, 1, w_tail.shape[2], F32)
        y_sb = _sb_attention(sb_qkv)
        y_gdn = _gdn(proj_g, tail, conv_w[l], gdn_a_log[l], gdn_dt_bias[l], gdn_norm_w[l])
        y_sg = _spatial_gating(tail, sg_norm_w[l], sg_norm_b[l], sg_w[l], sg_b[l])
        xt, h2 = _outproj_ln(xt, y_sb, y_gdn, y_sg, w_out_b[l], mod, ln1_w[l], ln1_b[l], alpha)
        idx_t, wgt_t, rank_t, cnt = _router(h2, w_router_t[l], router_bias[l])
        pos_t, tile_expert, seg_end = _plan(cnt, idx_t, rank_t, EXPERT_TM, n_tiles_pad)
        pos_flat = pos_t.reshape(-1)
        seg_end = seg_end[:, 0]
        tiles_used = seg_end[N_EXPERTS - 1:] // EXPERT_TM
        xs = _dispatch(pos_flat, seg_end, h2, n_rows, EXPERT_TM)
        os_rows = _experts(tile_expert.reshape(-1), tiles_used, xs, e1, e3, e2, l)
        xt, h = _combine(pos_flat, os_rows, xt, h2, wgt_t.T, s1[l], s3[l], s2[l], mod,
                         ln2_w[l], ln2_b[l], mod_next, alpha)
    return xt.reshape(B, T, D)
```

```python
import functools

import jax
import jax.numpy as jnp
from jax import lax
from jax.experimental import pallas as pl
from jax.experimental.pallas import tpu as pltpu

F32 = jnp.float32
BF16 = jnp.bfloat16
I32 = jnp.int32

LANES = 128
SUBLANES = 8
VMEM_LIMIT = 56 * 1024 * 1024

HEAD_DIM = 128
SB_HEADS = 4
GDN_HEADS = 8
SG_GROUPS = 4
SB_W = SB_HEADS * HEAD_DIM
GDN_W = GDN_HEADS * HEAD_DIM
SG_W = SG_GROUPS * HEAD_DIM
GDN_CHUNK = 64
CONV_W = 4
SG_CHUNK = 128
N_EXPERTS = 64
TOP_K = 8
N_GROUPS = 8
GROUP_SIZE = N_EXPERTS // N_GROUPS
TOPK_GROUPS = 4
ROUTED_SCALE = 2.5
LN_EPS = 1e-5

SB_TQ = 256
SB_TK = 256
SB_UNDERFLOW = 110.0
EXPERT_TM = 256
ROUTER_TM = 512
DISPATCH_TD = 256
COMBINE_TC = 128


def _cparams(sem):
    return pltpu.CompilerParams(dimension_semantics=sem, vmem_limit_bytes=VMEM_LIMIT)


def _sigmoid(x):
    return 1.0 / (1.0 + jnp.exp(-x))


def _silu(x):
    return x * _sigmoid(x)


def _softplus(x):
    return jnp.maximum(x, 0.0) + jnp.log(1.0 + jnp.exp(-jnp.abs(x)))


def _iota(shape, dim):
    return lax.broadcasted_iota(I32, shape, dim)


def _ada_kernel(c_ref, w_ref, b_ref, o_ref):
    cb = _silu(c_ref[...])
    tn = w_ref.shape[2]
    for j in range(tn // LANES):
        sl = slice(j * LANES, (j + 1) * LANES)
        o_ref[0, :, sl] = jnp.sum(w_ref[0, :, sl] * cb, axis=0, keepdims=True) + b_ref[0, :, sl]


def _ada_mod(c, w_ada, b_ada):
    L, D, N = w_ada.shape
    tn = 1536
    cb = jnp.broadcast_to(c.reshape(D, 1), (D, LANES))
    return pl.pallas_call(
        _ada_kernel,
        grid=(L, N // tn),
        in_specs=[pl.BlockSpec((D, LANES), lambda l, j: (0, 0)),
                  pl.BlockSpec((1, D, tn), lambda l, j: (l, 0, j)),
                  pl.BlockSpec((1, 1, tn), lambda l, j: (l, 0, j))],
        out_specs=pl.BlockSpec((1, 1, tn), lambda l, j: (l, 0, j)),
        out_shape=jax.ShapeDtypeStruct((L, 1, N), F32),
        compiler_params=_cparams(("arbitrary", "arbitrary")),
        name="ada_mod",
    )(cb, w_ada, b_ada.reshape(L, 1, N))


def _modulate_kernel(x_ref, sc_ref, sh_ref, o_ref):
    o_ref[...] = (x_ref[...] * (1.0 + sc_ref[...]) + sh_ref[...]).astype(o_ref.dtype)


def _modulate(x, mod, sc_idx, sh_idx):
    T, D = x.shape
    tm = 512
    return pl.pallas_call(
        _modulate_kernel,
        grid=(T // tm,),
        in_specs=[pl.BlockSpec((tm, D), lambda i: (i, 0)),
                  pl.BlockSpec((1, D), lambda i: (0, sc_idx)),
                  pl.BlockSpec((1, D), lambda i: (0, sh_idx))],
        out_specs=pl.BlockSpec((tm, D), lambda i: (i, 0)),
        out_shape=jax.ShapeDtypeStruct((T, D), BF16),
        compiler_params=_cparams(("arbitrary",)),
        name="modulate",
    )(x, mod, mod)


def _proj_kernel(x_ref, w_ref, o_ref, wb_ref):
    @pl.when(pl.program_id(1) == 0)
    def _():
        wb_ref[...] = w_ref[0].astype(BF16)

    o_ref[...] = jnp.dot(x_ref[...], wb_ref[...], preferred_element_type=F32).astype(o_ref.dtype)


def _proj(h, w, layer, col_block0, n_blocks, tn, out_dtype):
    T, K = h.shape
    tm = min(1024, T)
    return pl.pallas_call(
        _proj_kernel,
        grid=(n_blocks, T // tm),
        in_specs=[pl.BlockSpec((tm, K), lambda j, i: (i, 0)),
                  pl.BlockSpec((1, K, tn), lambda j, i: (layer, 0, j + col_block0))],
        out_specs=pl.BlockSpec((tm, tn), lambda j, i: (i, j)),
        out_shape=jax.ShapeDtypeStruct((T, n_blocks * tn), out_dtype),
        scratch_shapes=[pltpu.VMEM((K, tn), BF16)],
        compiler_params=_cparams(("arbitrary", "arbitrary")),
        name="in_proj",
    )(h, w)


def _sb_kernel(q_ref, k_ref, v_ref, o_ref, *, tq, tk, scale):
    i = pl.program_id(1)
    q = q_ref[...]
    tri = (_iota((tk, tk), 0) >= _iota((tk, tk), 1)).astype(BF16)

    def block(j, r, diagonal):
        ks = pl.multiple_of(j * tk, tk)
        k = k_ref[pl.ds(ks, tk), :]
        v = v_ref[pl.ds(ks, tk), :]
        z = lax.dot_general(q, k, (((1,), (1,)), ((), ())), preferred_element_type=F32) * scale
        sp = _softplus(z)
        if diagonal:
            causal = _iota((tq, tk), 1) < _iota((tq, tk), 0)
            sp = jnp.where(causal, sp, 0.0)
        sp_hi = sp.astype(BF16)
        sp_lo = (sp - sp_hi.astype(F32)).astype(BF16)
        c = (jnp.dot(sp_hi, tri, preferred_element_type=F32)
             + jnp.dot(sp_lo, tri, preferred_element_type=F32))
        a = jnp.exp(z - c - r)
        if diagonal:
            a = jnp.where(causal, a, 0.0)
        pv = jnp.dot(a.astype(BF16), v, preferred_element_type=F32)
        return pv, r + c[:, 0:1]

    acc, r = block(i, jnp.zeros((tq, 1), F32), True)

    def cond(carry):
        step, _, _, rmin = carry
        return jnp.logical_and(step <= i, rmin < SB_UNDERFLOW)

    def body(carry):
        step, acc, r, _ = carry
        pv, r = block(i - step, r, False)
        return step + 1, acc + pv, r, jnp.min(r)

    _, acc, _, _ = lax.while_loop(cond, body, (jnp.int32(1), acc, r, jnp.min(r)))
    o_ref[...] = acc.astype(o_ref.dtype)


def _sb_attention(qkv):
    T = qkv.shape[0]
    tq, tk = SB_TQ, SB_TK
    assert tq == tk
    kern = functools.partial(_sb_kernel, tq=tq, tk=tk, scale=HEAD_DIM ** -0.5)
    return pl.pallas_call(
        kern,
        grid=(SB_HEADS, T // tq),
        in_specs=[pl.BlockSpec((tq, HEAD_DIM), lambda h, i: (i, h)),
                  pl.BlockSpec((T, HEAD_DIM), lambda h, i: (0, SB_HEADS + h)),
                  pl.BlockSpec((T, HEAD_DIM), lambda h, i: (0, 2 * SB_HEADS + h))],
        out_specs=pl.BlockSpec((tq, HEAD_DIM), lambda h, i: (i, h)),
        out_shape=jax.ShapeDtypeStruct((T, SB_W), BF16),
        compiler_params=_cparams(("arbitrary", "arbitrary")),
        name="sb_attention",
    )(qkv, qkv, qkv)


def _gdn_kernel(qkv_ref, z_ref, ab_ref, cw_ref, alog_ref, dtb_ref, nw_ref, o_ref,
                tail_ref, state_ref):
    C = GDN_CHUNK
    Dh = HEAD_DIM
    step = pl.program_id(0)

    @pl.when(step == 0)
    def _():
        tail_ref[...] = jnp.zeros_like(tail_ref)
        state_ref[...] = jnp.zeros_like(state_ref)

    x = qkv_ref[...]
    ext = jnp.concatenate([tail_ref[...], x], axis=0)
    cw = cw_ref[...]
    conv = cw[CONV_W - 1:CONV_W, :] * x
    for s in range(1, CONV_W):
        conv = conv + cw[CONV_W - 1 - s:CONV_W - s, :] * ext[SUBLANES - s:SUBLANES - s + C, :]
    tail_ref[...] = x[C - SUBLANES:, :]
    act = _silu(conv)

    ab = ab_ref[...]
    g = -jnp.exp(alog_ref[...]) * _softplus(ab + dtb_ref[...])
    beta_all = _sigmoid(ab)
    ri = _iota((C, C), 0)
    ci = _iota((C, C), 1)
    tril = (ri >= ci)
    strict = (ri > ci)
    gc = jnp.dot(tril.astype(F32), g, preferred_element_type=F32,
                 precision=lax.Precision.HIGHEST)
    gc_t = jnp.transpose(gc)
    eye = (ri == ci).astype(F32)
    nw = nw_ref[...]
    zz = z_ref[...]

    H = range(GDN_HEADS)
    nt = (((1,), (1,)), ((), ()))
    mm = lambda a, b: jnp.dot(a, b, preferred_element_type=F32)
    qs = [act[:, h * Dh:(h + 1) * Dh] for h in H]
    ks = [act[:, GDN_W + h * Dh:GDN_W + (h + 1) * Dh] for h in H]
    vs = [act[:, 2 * GDN_W + h * Dh:2 * GDN_W + (h + 1) * Dh] for h in H]
    qs = [q * lax.rsqrt(jnp.sum(q * q, axis=-1, keepdims=True) + 1e-6) * (Dh ** -0.5) for q in qs]
    ks = [k * lax.rsqrt(jnp.sum(k * k, axis=-1, keepdims=True) + 1e-6) for k in ks]
    beta = [beta_all[:, SUBLANES + h:SUBLANES + h + 1] for h in H]
    gcol = [gc[:, h:h + 1] for h in H]
    grow = [gc_t[h:h + 1, :] for h in H]
    glast = [gc[C - 1:C, h:h + 1] for h in H]
    decay = [jnp.exp(jnp.where(tril, gcol[h] - grow[h], -jnp.inf)) for h in H]
    eg = [jnp.exp(gcol[h]) for h in H]
    kb = [ks[h] * beta[h] for h in H]
    kk = [lax.dot_general(kb[h], ks[h], nt, preferred_element_type=F32) for h in H]
    xm = [-jnp.where(strict, kk[h] * decay[h], 0.0) for h in H]
    tm = [eye + xm[h] for h in H]
    p = xm
    for _ in range(5):
        p = [mm(p[h], p[h]) for h in H]
        tm = [tm[h] + mm(tm[h], p[h]) for h in H]
    sol = [mm(tm[h], jnp.concatenate([vs[h] * beta[h], kb[h] * eg[h]], axis=1)) for h in H]
    qk = [lax.dot_general(qs[h], ks[h], nt, preferred_element_type=F32) for h in H]
    qk = [jnp.where(tril, qk[h] * decay[h], 0.0) for h in H]
    k_dec = [ks[h] * jnp.exp(glast[h] - gcol[h]) for h in H]
    state = [state_ref[h] for h in H]
    ws = [mm(jnp.concatenate([sol[h][:, Dh:], qs[h] * eg[h]], axis=0), state[h]) for h in H]
    v_new = [sol[h][:, :Dh] - ws[h][:C] for h in H]
    o = [ws[h][C:] + mm(qk[h], v_new[h]) for h in H]
    for h in H:
        state_ref[h] = state[h] * jnp.exp(glast[h]) + lax.dot_general(
            k_dec[h], v_new[h], (((0,), (0,)), ((), ())), preferred_element_type=F32)
    for h in H:
        hs = slice(h * Dh, (h + 1) * Dh)
        on = o[h] * lax.rsqrt(jnp.mean(o[h] * o[h], axis=-1, keepdims=True) + 1e-6)
        o_ref[:, hs] = (on * nw * _silu(zz[:, hs])).astype(o_ref.dtype)


def _gdn(proj_g, tail, conv_w, a_log, dt_bias, norm_w):
    T = proj_g.shape[0]
    C = GDN_CHUNK
    pad = lambda t: jnp.zeros((1, LANES), F32).at[0, :GDN_HEADS].set(t)
    ab_block = tail.shape[1] // LANES - 1
    return pl.pallas_call(
        _gdn_kernel,
        grid=(T // C,),
        in_specs=[pl.BlockSpec((C, 3 * GDN_W), lambda i: (i, 0)),
                  pl.BlockSpec((C, GDN_W), lambda i: (i, 3)),
                  pl.BlockSpec((C, LANES), lambda i: (i, ab_block)),
                  pl.BlockSpec((CONV_W, 3 * GDN_W), lambda i: (0, 0)),
                  pl.BlockSpec((1, LANES), lambda i: (0, 0)),
                  pl.BlockSpec((1, LANES), lambda i: (0, 0)),
                  pl.BlockSpec((1, HEAD_DIM), lambda i: (0, 0))],
        out_specs=pl.BlockSpec((C, GDN_W), lambda i: (i, 0)),
        out_shape=jax.ShapeDtypeStruct((T, GDN_W), BF16),
        scratch_shapes=[pltpu.VMEM((SUBLANES, 3 * GDN_W), F32),
                        pltpu.VMEM((GDN_HEADS, HEAD_DIM, HEAD_DIM), F32)],
        compiler_params=_cparams(("arbitrary",)),
        name="gdn",
    )(proj_g, proj_g, tail, conv_w, pad(a_log), pad(dt_bias), norm_w.reshape(1, HEAD_DIM))


def _gelu(x):
    return 0.5 * x * (1.0 + lax.erf(x * (2.0 ** -0.5)))


def _sg_kernel(u_ref, v_ref, nw_ref, nb_ref, w_ref, b_ref, o_ref):
    C = SG_CHUNK
    causal = _iota((C, C), 0) >= _iota((C, C), 1)
    for g in range(SG_GROUPS):
        gs = slice(g * HEAD_DIM, (g + 1) * HEAD_DIM)
        u = _gelu(u_ref[:, gs])
        v = _gelu(v_ref[:, gs])
        mu = jnp.mean(v, axis=-1, keepdims=True)
        var = jnp.mean(jnp.square(v - mu), axis=-1, keepdims=True)
        vn = (v - mu) * lax.rsqrt(var + LN_EPS) * nw_ref[:, gs] + nb_ref[:, gs]
        ws = jnp.where(causal, w_ref[g], 0.0)
        mixed = jnp.dot(ws, vn, preferred_element_type=F32) + b_ref[:, g:g + 1]
        o_ref[:, gs] = (u * mixed).astype(o_ref.dtype)


def _spatial_gating(tail, norm_w, norm_b, sg_w, sg_b):
    T = tail.shape[0]
    C = SG_CHUNK
    return pl.pallas_call(
        _sg_kernel,
        grid=(T // C,),
        in_specs=[pl.BlockSpec((C, SG_W), lambda i: (i, 0)),
                  pl.BlockSpec((C, SG_W), lambda i: (i, 1)),
                  pl.BlockSpec((1, SG_W), lambda i: (0, 0)),
                  pl.BlockSpec((1, SG_W), lambda i: (0, 0)),
                  pl.BlockSpec((SG_GROUPS, C, C), lambda i: (0, 0, 0)),
                  pl.BlockSpec((C, SG_GROUPS), lambda i: (0, 0))],
        out_specs=pl.BlockSpec((C, SG_W), lambda i: (i, 0)),
        out_shape=jax.ShapeDtypeStruct((T, SG_W), BF16),
        compiler_params=_cparams(("arbitrary",)),
        name="spatial_gating",
    )(tail, tail, norm_w.reshape(1, SG_W), norm_b.reshape(1, SG_W), sg_w, sg_b.T)


def _layer_norm_rows(r, w, b):
    mu = jnp.mean(r, axis=-1, keepdims=True)
    d = r - mu
    var = jnp.mean(d * d, axis=-1, keepdims=True)
    return d * lax.rsqrt(var + LN_EPS) * w + b


def _outproj_kernel(x_ref, ysb_ref, ygdn_ref, ysg_ref, w_ref, gt_ref, lw_ref, lb_ref,
                    sc_ref, sh_ref, xo_ref, ho_ref, *, alpha):
    y = jnp.dot(ysb_ref[...], w_ref[0:SB_W, :], preferred_element_type=F32)
    y = y + jnp.dot(ygdn_ref[...], w_ref[SB_W:SB_W + GDN_W, :], preferred_element_type=F32)
    y = y + jnp.dot(ysg_ref[...], w_ref[SB_W + GDN_W:, :], preferred_element_type=F32)
    r = alpha * x_ref[...] + (1.0 + gt_ref[...]) * y
    xn = _layer_norm_rows(r, lw_ref[...], lb_ref[...])
    xo_ref[...] = xn
    ho_ref[...] = xn * (1.0 + sc_ref[...]) + sh_ref[...]


def _outproj_ln(x, y_sb, y_gdn, y_sg, w_out_bf16, mod, ln_w, ln_b, alpha):
    T, D = x.shape
    tm = 256
    row = lambda idx: pl.BlockSpec((1, D), lambda i: (0, idx))
    return pl.pallas_call(
        functools.partial(_outproj_kernel, alpha=alpha),
        grid=(T // tm,),
        in_specs=[pl.BlockSpec((tm, D), lambda i: (i, 0)),
                  pl.BlockSpec((tm, SB_W), lambda i: (i, 0)),
                  pl.BlockSpec((tm, GDN_W), lambda i: (i, 0)),
                  pl.BlockSpec((tm, SG_W), lambda i: (i, 0)),
                  pl.BlockSpec((D, D), lambda i: (0, 0)),
                  row(2), row(0), row(0), row(4), row(3)],
        out_specs=[pl.BlockSpec((tm, D), lambda i: (i, 0)),
                   pl.BlockSpec((tm, D), lambda i: (i, 0))],
        out_shape=[jax.ShapeDtypeStruct((T, D), F32), jax.ShapeDtypeStruct((T, D), F32)],
        compiler_params=_cparams(("arbitrary",)),
        name="outproj_ln",
    )(x, y_sb, y_gdn, y_sg, w_out_bf16, mod, ln_w.reshape(1, D), ln_b.reshape(1, D), mod, mod)


def _first_max(vals, idx, axis):
    m = jnp.max(vals, axis=axis, keepdims=True)
    big = jnp.iinfo(jnp.int32).max
    first = jnp.min(jnp.where(vals == m, idx, big), axis=axis, keepdims=True)
    return m, first


def _router_kernel(h_ref, wt_ref, b_ref, idx_ref, wgt_ref, rank_ref, cnt_ref, carry_ref):
    E, G, S = N_EXPERTS, N_GROUPS, GROUP_SIZE
    tm = h_ref.shape[0]
    step = pl.program_id(0)

    @pl.when(step == 0)
    def _():
        carry_ref[...] = jnp.zeros_like(carry_ref)

    h = h_ref[...]
    h_hi = h.astype(BF16)
    h_lo = (h - h_hi.astype(F32)).astype(BF16)
    wt = wt_ref[...]
    w_hi = wt.astype(BF16)
    w_lo = (wt - w_hi.astype(F32)).astype(BF16)
    nt = (((1,), (1,)), ((), ()))
    logits = (lax.dot_general(w_hi, h_hi, nt, preferred_element_type=F32)
              + lax.dot_general(w_hi, h_lo, nt, preferred_element_type=F32)
              + lax.dot_general(w_lo, h_hi, nt, preferred_element_type=F32))
    scores = _sigmoid(logits)
    sel = scores + b_ref[:, 0:1]

    sel3 = sel.reshape(G, S, tm)
    e_in_g = _iota((G, S, tm), 1)
    m1, i1 = _first_max(sel3, e_in_g, 1)
    m2 = jnp.max(jnp.where(e_in_g == i1, -jnp.inf, sel3), axis=1, keepdims=True)
    gs = m1 + m2

    g_idx = _iota((G, 1, tm), 0)
    gmask = jnp.zeros((G, 1, tm), jnp.bool_)
    for _ in range(TOPK_GROUPS):
        _, gi = _first_max(gs, g_idx, 0)
        hit = g_idx == gi
        gmask = jnp.logical_or(gmask, hit)
        gs = jnp.where(hit, -jnp.inf, gs)
    cur = jnp.where(gmask, sel3, -jnp.inf).reshape(E, tm)

    e_idx = _iota((E, tm), 0)
    chosen = jnp.zeros((E, tm), jnp.bool_)
    idx_rows = []
    score_rows = []
    for _ in range(TOP_K):
        _, ei = _first_max(cur, e_idx, 0)
        hit = e_idx == ei
        chosen = jnp.logical_or(chosen, hit)
        cur = jnp.where(hit, -jnp.inf, cur)
        idx_rows.append(ei)
        score_rows.append(jnp.sum(jnp.where(hit, scores, 0.0), axis=0, keepdims=True))
    top_idx = jnp.concatenate(idx_rows, axis=0)
    top_sc = jnp.concatenate(score_rows, axis=0)
    idx_ref[...] = top_idx
    wgt_ref[...] = top_sc / jnp.sum(top_sc, axis=0, keepdims=True) * ROUTED_SCALE

    chosen_b = chosen.astype(BF16)
    before = (_iota((tm, tm), 0) < _iota((tm, tm), 1)).astype(BF16)
    excl = jnp.dot(chosen_b, before, preferred_element_type=F32) + carry_ref[:, 0:1]
    rank_rows = [jnp.sum(jnp.where(e_idx == idx_rows[k], excl, 0.0), axis=0, keepdims=True)
                 for k in range(TOP_K)]
    rank_ref[...] = jnp.concatenate(rank_rows, axis=0).astype(I32)
    carry_ref[...] = carry_ref[...] + jnp.sum(chosen.astype(F32), axis=1, keepdims=True)
    cnt_ref[...] = carry_ref[...]


def _router(h2, w_router_t, router_bias):
    T, D = h2.shape
    tm = ROUTER_TM
    E, K = N_EXPERTS, TOP_K
    bias = jnp.broadcast_to(router_bias.reshape(E, 1), (E, LANES))
    tok = lambda: pl.BlockSpec((K, tm), lambda i: (0, i))
    return pl.pallas_call(
        _router_kernel,
        grid=(T // tm,),
        in_specs=[pl.BlockSpec((tm, D), lambda i: (i, 0)),
                  pl.BlockSpec((E, D), lambda i: (0, 0)),
                  pl.BlockSpec((E, LANES), lambda i: (0, 0))],
        out_specs=[tok(), tok(), tok(), pl.BlockSpec((E, LANES), lambda i: (0, 0))],
        out_shape=[jax.ShapeDtypeStruct((K, T), I32), jax.ShapeDtypeStruct((K, T), F32),
                   jax.ShapeDtypeStruct((K, T), I32), jax.ShapeDtypeStruct((E, LANES), F32)],
        scratch_shapes=[pltpu.VMEM((E, LANES), F32)],
        compiler_params=_cparams(("arbitrary",)),
        name="router",
    )(h2, w_router_t, bias)


def _plan_kernel(cnt_ref, idx_ref, rank_ref, pos_ref, texp_ref, end_ref, *, tile, n_tiles):
    E = N_EXPERTS
    tm = idx_ref.shape[1]
    cnt = cnt_ref[...]
    ntile = jnp.floor((cnt + (tile - 1)) * (1.0 / tile))
    lower = (_iota((E, E), 0) >= _iota((E, E), 1)).astype(F32)
    tile_end = jnp.dot(lower, ntile, preferred_element_type=F32)
    row_start = (tile_end - ntile) * tile
    e_idx = _iota((E, tm), 0)
    start_col = row_start[:, 0:1]
    for k in range(TOP_K):
        base = jnp.sum(jnp.where(e_idx == idx_ref[k:k + 1, :], start_col, 0.0), axis=0, keepdims=True)
        pos_ref[k:k + 1, :] = base.astype(I32) + rank_ref[k:k + 1, :]
    tile_i = _iota((E, n_tiles), 1).astype(F32)
    texp = jnp.sum((tile_end[:, 0:1] <= tile_i).astype(F32), axis=0, keepdims=True)
    texp_ref[...] = jnp.minimum(texp, E - 1.0).astype(I32)
    end_ref[...] = (tile_end * tile).astype(I32)


def _plan(cnt, idx_t, rank_t, tile, n_tiles):
    K, T = idx_t.shape
    E = N_EXPERTS
    tm = min(2048, T)
    tok = lambda: pl.BlockSpec((K, tm), lambda i: (0, i))
    return pl.pallas_call(
        functools.partial(_plan_kernel, tile=tile, n_tiles=n_tiles),
        grid=(T // tm,),
        in_specs=[pl.BlockSpec((E, LANES), lambda i: (0, 0)), tok(), tok()],
        out_specs=[tok(), pl.BlockSpec((1, n_tiles), lambda i: (0, 0)),
                   pl.BlockSpec((E, LANES), lambda i: (0, 0))],
        out_shape=[jax.ShapeDtypeStruct((K, T), I32), jax.ShapeDtypeStruct((1, n_tiles), I32),
                   jax.ShapeDtypeStruct((E, LANES), I32)],
        compiler_params=_cparams(("arbitrary",)),
        name="route_plan",
    )(cnt, idx_t, rank_t)


def _invert_kernel(pos_ref, inv_ref, *, n_pairs, n_rows, tile):
    def init(r, c):
        inv_ref[r] = n_pairs + (r & (tile - 1))
        return c

    lax.fori_loop(0, n_rows, init, 0, unroll=8)

    def scatter(q, c):
        inv_ref[pos_ref[q]] = q
        return c

    lax.fori_loop(0, n_pairs, scatter, 0, unroll=8)


def _invert(pos_flat, n_rows, tile):
    n_pairs = pos_flat.shape[0]
    return pl.pallas_call(
        functools.partial(_invert_kernel, n_pairs=n_pairs, n_rows=n_rows, tile=tile),
        in_specs=[pl.BlockSpec(memory_space=pltpu.SMEM)],
        out_specs=pl.BlockSpec(memory_space=pltpu.SMEM),
        out_shape=jax.ShapeDtypeStruct((n_rows,), I32),
        name="route_invert",
    )(pos_flat)


def _expert_kernel(texp_ref, used_ref, inv_ref, h_ref, w1_ref, w3_ref, w2_ref, ys_ref,
                   xbuf, obuf, gsem, ssem, *, tm, n_tok):
    i = pl.program_id(0)
    used = used_ref[0]
    slot = lax.rem(i, 2)
    other = 1 - slot
    n_pairs = n_tok * TOP_K

    def gather_start(tile_idx, s):
        base = tile_idx * tm
        for j in range(tm):
            tok = inv_ref[base + j] & (n_tok - 1)
            pltpu.make_async_copy(h_ref.at[pl.ds(tok, 1)], xbuf.at[s, pl.ds(j, 1)], gsem.at[s]).start()

    def gather_wait(s):
        pltpu.make_async_copy(h_ref.at[pl.ds(0, tm)], xbuf.at[s], gsem.at[s]).wait()

    def scatter_start(tile_idx, s, dummy):
        base = tile_idx * tm
        for j in range(tm):
            dst = jnp.where(dummy, n_pairs + j, inv_ref[base + j])
            pltpu.make_async_copy(obuf.at[s, pl.ds(j, 1)], ys_ref.at[pl.ds(dst, 1)], ssem.at[s]).start()

    def scatter_wait(s):
        pltpu.make_async_copy(obuf.at[s], ys_ref.at[pl.ds(0, tm)], ssem.at[s]).wait()

    @pl.when(i == 0)
    def _():
        obuf[1] = jnp.zeros((tm, obuf.shape[2]), obuf.dtype)
        gather_start(0, 0)

    @pl.when(jnp.logical_and(i >= 1, i < used))
    def _():
        scatter_wait(slot)

    @pl.when(i < used)
    def _():
        gather_wait(slot)
        gather_start(jnp.minimum(i + 1, used - 1), other)
        scatter_start(jnp.maximum(i - 1, 0), other, i == 0)
        x = xbuf[slot].astype(BF16)
        a = _silu(jnp.dot(x, w1_ref[0].astype(BF16), preferred_element_type=F32))
        a = a * jnp.dot(x, w3_ref[0].astype(BF16), preferred_element_type=F32)
        obuf[slot] = jnp.dot(a.astype(BF16), w2_ref[0].astype(BF16), preferred_element_type=F32)

    @pl.when(i == used - 1)
    def _():
        scatter_wait(other)
        scatter_start(i, slot, False)
        gather_wait(other)
        scatter_wait(slot)


def _experts(tile_expert, tiles_used, inv, h2, w1, w3, w2, layer):
    T, D = h2.shape
    F = w1.shape[-1]
    tm = EXPERT_TM
    assert T & (T - 1) == 0
    n_tiles = inv.shape[0] // tm
    wspec = lambda a, b: pl.BlockSpec((None, 1, a, b), lambda i, te, nu, iv: (layer, te[i], 0, 0))
    return pl.pallas_call(
        functools.partial(_expert_kernel, tm=tm, n_tok=T),
        grid_spec=pltpu.PrefetchScalarGridSpec(
            num_scalar_prefetch=3,
            grid=(n_tiles,),
            in_specs=[pl.BlockSpec(memory_space=pl.ANY), wspec(D, F), wspec(D, F), wspec(F, D)],
            out_specs=pl.BlockSpec(memory_space=pl.ANY),
            scratch_shapes=[pltpu.VMEM((2, tm, D), F32), pltpu.VMEM((2, tm, D), F32),
                            pltpu.SemaphoreType.DMA((2,)), pltpu.SemaphoreType.DMA((2,))]),
        out_shape=jax.ShapeDtypeStruct((T * TOP_K + tm, D), F32),
        compiler_params=_cparams(("arbitrary",)),
        name="moe_experts",
    )(tile_expert, tiles_used, inv, h2, w1, w3, w2)


def _combine_kernel(*refs, alpha):
    y_refs = refs[:TOP_K]
    (x_ref, h_ref, g_ref, s1_ref, s3_ref, s2_ref, gt_ref, lw_ref, lb_ref, sc_ref, sh_ref,
     xo_ref, ho_ref) = refs[TOP_K:]
    hb = h_ref[...].astype(BF16)
    a = _silu(jnp.dot(hb, s1_ref[...], preferred_element_type=F32))
    a = a * jnp.dot(hb, s3_ref[...], preferred_element_type=F32)
    y = jnp.dot(a.astype(BF16), s2_ref[...], preferred_element_type=F32)
    g = g_ref[...]
    for k in range(TOP_K):
        y = y + g[:, k:k + 1] * y_refs[k][...]
    r = alpha * x_ref[...] + (1.0 + gt_ref[...]) * y
    xn = _layer_norm_rows(r, lw_ref[...], lb_ref[...])
    xo_ref[...] = xn
    ho_ref[...] = (xn * (1.0 + sc_ref[...]) + sh_ref[...]).astype(ho_ref.dtype)


def _combine(ys, x, h2, gates, s1, s3, s2, mod, ln_w, ln_b, mod_next, alpha):
    T, D = x.shape
    F = s1.shape[-1]
    tc = COMBINE_TC
    nb = T // tc
    row = lambda idx: pl.BlockSpec((1, D), lambda i: (0, idx))
    full = lambda a, b: pl.BlockSpec((a, b), lambda i: (0, 0))
    tok = lambda: pl.BlockSpec((tc, D), lambda i: (i, 0))
    pair = lambda k: pl.BlockSpec((tc, D), lambda i: (k * nb + i, 0))
    return pl.pallas_call(
        functools.partial(_combine_kernel, alpha=alpha),
        grid=(nb,),
        in_specs=[pair(k) for k in range(TOP_K)] + [
            tok(), tok(), pl.BlockSpec((tc, TOP_K), lambda i: (i, 0)),
            full(D, F), full(D, F), full(F, D), row(5), row(0), row(0), row(1), row(0)],
        out_specs=[tok(), tok()],
        out_shape=[jax.ShapeDtypeStruct((T, D), F32), jax.ShapeDtypeStruct((T, D), BF16)],
        compiler_params=_cparams(("arbitrary",)),
        name="moe_combine",
    )(*([ys] * TOP_K), x, h2, gates, s1, s3, s2, mod, ln_w.reshape(1, D), ln_b.reshape(1, D),
      mod_next, mod_next)


def kernel(x, c, w_ada, b_ada, w_in, conv_w, gdn_a_log, gdn_dt_bias, gdn_norm_w, sg_norm_w, sg_norm_b, sg_w, sg_b, w_out, ln1_w, ln1_b, w_router, router_bias, exp_w1, exp_w3, exp_w2, sh_w1, sh_w3, sh_w2, ln2_w, ln2_b):
    B, T, D = x.shape
    assert B == 1
    L = w_ada.shape[0]
    alpha = (2.0 * L) ** 0.25
    main_cols = 3 * SB_W + 4 * GDN_W
    assert main_cols % 512 == 0 and w_in.shape[2] == main_cols + 2 * GDN_HEADS + 2 * SG_W
    n_tiles = (T * TOP_K) // EXPERT_TM + N_EXPERTS
    n_rows = n_tiles * EXPERT_TM
    n_tiles_pad = -(-n_tiles // LANES) * LANES

    xt = x.reshape(T, D)
    mod_all = _ada_mod(c, w_ada, b_ada)

    ab_cols = w_in[:, :, main_cols:main_cols + 2 * GDN_HEADS]
    ab_cols = jnp.pad(ab_cols, ((0, 0), (0, 0), (0, LANES - 2 * GDN_HEADS)))
    w_tail = jnp.concatenate([w_in[:, :, main_cols + 2 * GDN_HEADS:], ab_cols], axis=-1)
    w_out_b = w_out.astype(BF16)
    w_router_t = jnp.swapaxes(w_router, 1, 2)
    s1, s3, s2 = sh_w1.astype(BF16), sh_w3.astype(BF16), sh_w2.astype(BF16)

    h = _modulate(xt, mod_all[0], 1, 0)
    for l in range(L):
        mod = mod_all[l]
        mod_next = mod_all[min(l + 1, L - 1)]
        sb_qkv = _proj(h, w_in, l, 0, 3 * SB_W // 512, 512, BF16)
        proj_g = _proj(h, w_in, l, 3 * SB_W // 512, 4 * GDN_W // 512, 512, F32)
        tail = _proj(h, w_tail, l, 0, 1, w_tail.shape[2], F32)
        y_sb = _sb_attention(sb_qkv)
        y_gdn = _gdn(proj_g, tail, conv_w[l], gdn_a_log[l], gdn_dt_bias[l], gdn_norm_w[l])
        y_sg = _spatial_gating(tail, sg_norm_w[l], sg_norm_b[l], sg_w[l], sg_b[l])
        xt, h2 = _outproj_ln(xt, y_sb, y_gdn, y_sg, w_out_b[l], mod, ln1_w[l], ln1_b[l], alpha)
        idx_t, wgt_t, rank_t, cnt = _router(h2, w_router_t[l], router_bias[l])
        pos_t, tile_expert, seg_end = _plan(cnt, idx_t, rank_t, EXPERT_TM, n_tiles_pad)
        inv = _invert(pos_t.reshape(-1), n_rows, EXPERT_TM)
        tiles_used = seg_end[N_EXPERTS - 1:, 0] // EXPERT_TM
        ys = _experts(tile_expert.reshape(-1), tiles_used, inv, h2, exp_w1, exp_w3, exp_w2, l)
        xt, h = _combine(ys, xt, h2, wgt_t.T, s1[l], s3[l], s2[l], mod, ln2_w[l], ln2_b[l],
                         mod_next, alpha)
    return xt.reshape(B, T, D)
```

```python
import functools

import jax
import jax.numpy as jnp
from jax import lax
from jax.experimental import pallas as pl
from jax.experimental.pallas import tpu as pltpu

F32 = jnp.float32
BF16 = jnp.bfloat16
I32 = jnp.int32

LANES = 128
SUBLANES = 8
VMEM_LIMIT = 56 * 1024 * 1024

HEAD_DIM = 128
SB_HEADS = 4
GDN_HEADS = 8
SG_GROUPS = 4
SB_W = SB_HEADS * HEAD_DIM
GDN_W = GDN_HEADS * HEAD_DIM
SG_W = SG_GROUPS * HEAD_DIM
GDN_CHUNK = 64
CONV_W = 4
SG_CHUNK = 128
N_EXPERTS = 64
TOP_K = 8
N_GROUPS = 8
GROUP_SIZE = N_EXPERTS // N_GROUPS
TOPK_GROUPS = 4
ROUTED_SCALE = 2.5
LN_EPS = 1e-5

SB_TQ = 256
SB_TK = 256
SB_UNDERFLOW = 110.0
EXPERT_TM = 256
ROUTER_TM = 512
EXPERT_K_CHUNKS = 8
COMBINE_TC = 128


def _cparams(sem):
    return pltpu.CompilerParams(dimension_semantics=sem, vmem_limit_bytes=VMEM_LIMIT)


def _sigmoid(x):
    return 1.0 / (1.0 + jnp.exp(-x))


def _silu(x):
    return x * _sigmoid(x)


def _softplus(x):
    return jnp.maximum(x, 0.0) + jnp.log(1.0 + jnp.exp(-jnp.abs(x)))


def _iota(shape, dim):
    return lax.broadcasted_iota(I32, shape, dim)


def _ada_kernel(c_ref, w_ref, b_ref, o_ref):
    cb = _silu(c_ref[...])
    tn = w_ref.shape[2]
    for j in range(tn // LANES):
        sl = slice(j * LANES, (j + 1) * LANES)
        o_ref[0, :, sl] = jnp.sum(w_ref[0, :, sl] * cb, axis=0, keepdims=True) + b_ref[0, :, sl]


def _ada_mod(c, w_ada, b_ada):
    L, D, N = w_ada.shape
    tn = 1536
    cb = jnp.broadcast_to(c.reshape(D, 1), (D, LANES))
    return pl.pallas_call(
        _ada_kernel,
        grid=(L, N // tn),
        in_specs=[pl.BlockSpec((D, LANES), lambda l, j: (0, 0)),
                  pl.BlockSpec((1, D, tn), lambda l, j: (l, 0, j)),
                  pl.BlockSpec((1, 1, tn), lambda l, j: (l, 0, j))],
        out_specs=pl.BlockSpec((1, 1, tn), lambda l, j: (l, 0, j)),
        out_shape=jax.ShapeDtypeStruct((L, 1, N), F32),
        compiler_params=_cparams(("arbitrary", "arbitrary")),
        name="ada_mod",
    )(cb, w_ada, b_ada.reshape(L, 1, N))


def _modulate_kernel(x_ref, sc_ref, sh_ref, o_ref):
    o_ref[...] = (x_ref[...] * (1.0 + sc_ref[...]) + sh_ref[...]).astype(o_ref.dtype)


def _modulate(x, mod, sc_idx, sh_idx):
    T, D = x.shape
    tm = 512
    return pl.pallas_call(
        _modulate_kernel,
        grid=(T // tm,),
        in_specs=[pl.BlockSpec((tm, D), lambda i: (i, 0)),
                  pl.BlockSpec((1, D), lambda i: (0, sc_idx)),
                  pl.BlockSpec((1, D), lambda i: (0, sh_idx))],
        out_specs=pl.BlockSpec((tm, D), lambda i: (i, 0)),
        out_shape=jax.ShapeDtypeStruct((T, D), BF16),
        compiler_params=_cparams(("arbitrary",)),
        name="modulate",
    )(x, mod, mod)


def _proj_kernel(x_ref, w_ref, o_ref, wb_ref):
    @pl.when(pl.program_id(1) == 0)
    def _():
        wb_ref[...] = w_ref[0].astype(BF16)

    o_ref[...] = jnp.dot(x_ref[...], wb_ref[...], preferred_element_type=F32).astype(o_ref.dtype)


def _proj(h, w, layer, col_block0, n_blocks, tn, out_dtype):
    T, K = h.shape
    tm = min(1024, T)
    return pl.pallas_call(
        _proj_kernel,
        grid=(n_blocks, T // tm),
        in_specs=[pl.BlockSpec((tm, K), lambda j, i: (i, 0)),
                  pl.BlockSpec((1, K, tn), lambda j, i: (layer, 0, j + col_block0))],
        out_specs=pl.BlockSpec((tm, tn), lambda j, i: (i, j)),
        out_shape=jax.ShapeDtypeStruct((T, n_blocks * tn), out_dtype),
        scratch_shapes=[pltpu.VMEM((K, tn), BF16)],
        compiler_params=_cparams(("arbitrary", "arbitrary")),
        name="in_proj",
    )(h, w)


def _sb_kernel(q_ref, k_ref, v_ref, o_ref, *, tq, tk, scale):
    i = pl.program_id(1)
    q = q_ref[...]
    tri = (_iota((tk, tk), 0) >= _iota((tk, tk), 1)).astype(BF16)

    def block(j, r, diagonal):
        ks = pl.multiple_of(j * tk, tk)
        k = k_ref[pl.ds(ks, tk), :]
        v = v_ref[pl.ds(ks, tk), :]
        z = lax.dot_general(q, k, (((1,), (1,)), ((), ())), preferred_element_type=F32) * scale
        sp = _softplus(z)
        if diagonal:
            causal = _iota((tq, tk), 1) < _iota((tq, tk), 0)
            sp = jnp.where(causal, sp, 0.0)
        sp_hi = sp.astype(BF16)
        sp_lo = (sp - sp_hi.astype(F32)).astype(BF16)
        c = (jnp.dot(sp_hi, tri, preferred_element_type=F32)
             + jnp.dot(sp_lo, tri, preferred_element_type=F32))
        a = jnp.exp(z - c - r)
        if diagonal:
            a = jnp.where(causal, a, 0.0)
        pv = jnp.dot(a.astype(BF16), v, preferred_element_type=F32)
        return pv, r + c[:, 0:1]

    acc, r = block(i, jnp.zeros((tq, 1), F32), True)

    def cond(carry):
        step, _, _, rmin = carry
        return jnp.logical_and(step <= i, rmin < SB_UNDERFLOW)

    def body(carry):
        step, acc, r, _ = carry
        pv, r = block(i - step, r, False)
        return step + 1, acc + pv, r, jnp.min(r)

    _, acc, _, _ = lax.while_loop(cond, body, (jnp.int32(1), acc, r, jnp.min(r)))
    o_ref[...] = acc.astype(o_ref.dtype)


def _sb_attention(qkv):
    T = qkv.shape[0]
    tq, tk = SB_TQ, SB_TK
    assert tq == tk
    kern = functools.partial(_sb_kernel, tq=tq, tk=tk, scale=HEAD_DIM ** -0.5)
    return pl.pallas_call(
        kern,
        grid=(SB_HEADS, T // tq),
        in_specs=[pl.BlockSpec((tq, HEAD_DIM), lambda h, i: (i, h)),
                  pl.BlockSpec((T, HEAD_DIM), lambda h, i: (0, SB_HEADS + h)),
                  pl.BlockSpec((T, HEAD_DIM), lambda h, i: (0, 2 * SB_HEADS + h))],
        out_specs=pl.BlockSpec((tq, HEAD_DIM), lambda h, i: (i, h)),
        out_shape=jax.ShapeDtypeStruct((T, SB_W), BF16),
        compiler_params=_cparams(("arbitrary", "arbitrary")),
        name="sb_attention",
    )(qkv, qkv, qkv)


def _gdn_kernel(qkv_ref, z_ref, ab_ref, cw_ref, alog_ref, dtb_ref, nw_ref, o_ref,
                tail_ref, state_ref):
    C = GDN_CHUNK
    Dh = HEAD_DIM
    step = pl.program_id(0)

    @pl.when(step == 0)
    def _():
        tail_ref[...] = jnp.zeros_like(tail_ref)
        state_ref[...] = jnp.zeros_like(state_ref)

    x = qkv_ref[...]
    ext = jnp.concatenate([tail_ref[...], x], axis=0)
    cw = cw_ref[...]
    conv = cw[CONV_W - 1:CONV_W, :] * x
    for s in range(1, CONV_W):
        conv = conv + cw[CONV_W - 1 - s:CONV_W - s, :] * ext[SUBLANES - s:SUBLANES - s + C, :]
    tail_ref[...] = x[C - SUBLANES:, :]
    act = _silu(conv)

    ab = ab_ref[...]
    g = -jnp.exp(alog_ref[...]) * _softplus(ab + dtb_ref[...])
    beta_all = _sigmoid(ab)
    ri = _iota((C, C), 0)
    ci = _iota((C, C), 1)
    tril = (ri >= ci)
    strict = (ri > ci)
    gc = jnp.dot(tril.astype(F32), g, preferred_element_type=F32,
                 precision=lax.Precision.HIGHEST)
    gc_t = jnp.transpose(gc)
    eye = (ri == ci).astype(F32)
    nw = nw_ref[...]
    zz = z_ref[...]

    H = range(GDN_HEADS)
    nt = (((1,), (1,)), ((), ()))
    mm = lambda a, b: jnp.dot(a, b, preferred_element_type=F32)
    qs = [act[:, h * Dh:(h + 1) * Dh] for h in H]
    ks = [act[:, GDN_W + h * Dh:GDN_W + (h + 1) * Dh] for h in H]
    vs = [act[:, 2 * GDN_W + h * Dh:2 * GDN_W + (h + 1) * Dh] for h in H]
    qs = [q * lax.rsqrt(jnp.sum(q * q, axis=-1, keepdims=True) + 1e-6) * (Dh ** -0.5) for q in qs]
    ks = [k * lax.rsqrt(jnp.sum(k * k, axis=-1, keepdims=True) + 1e-6) for k in ks]
    beta = [beta_all[:, SUBLANES + h:SUBLANES + h + 1] for h in H]
    gcol = [gc[:, h:h + 1] for h in H]
    grow = [gc_t[h:h + 1, :] for h in H]
    glast = [gc[C - 1:C, h:h + 1] for h in H]
    decay = [jnp.exp(jnp.where(tril, gcol[h] - grow[h], -jnp.inf)) for h in H]
    eg = [jnp.exp(gcol[h]) for h in H]
    kb = [ks[h] * beta[h] for h in H]
    kk = [lax.dot_general(kb[h], ks[h], nt, preferred_element_type=F32) for h in H]
    xm = [-jnp.where(strict, kk[h] * decay[h], 0.0) for h in H]
    tm = [eye + xm[h] for h in H]
    p = xm
    for _ in range(5):
        p = [mm(p[h], p[h]) for h in H]
        tm = [tm[h] + mm(tm[h], p[h]) for h in H]
    sol = [mm(tm[h], jnp.concatenate([vs[h] * beta[h], kb[h] * eg[h]], axis=1)) for h in H]
    qk = [lax.dot_general(qs[h], ks[h], nt, preferred_element_type=F32) for h in H]
    qk = [jnp.where(tril, qk[h] * decay[h], 0.0) for h in H]
    k_dec = [ks[h] * jnp.exp(glast[h] - gcol[h]) for h in H]
    state = [state_ref[h] for h in H]
    ws = [mm(jnp.concatenate([sol[h][:, Dh:], qs[h] * eg[h]], axis=0), state[h]) for h in H]
    v_new = [sol[h][:, :Dh] - ws[h][:C] for h in H]
    o = [ws[h][C:] + mm(qk[h], v_new[h]) for h in H]
    for h in H:
        state_ref[h] = state[h] * jnp.exp(glast[h]) + lax.dot_general(
            k_dec[h], v_new[h], (((0,), (0,)), ((), ())), preferred_element_type=F32)
    for h in H:
        hs = slice(h * Dh, (h + 1) * Dh)
        on = o[h] * lax.rsqrt(jnp.mean(o[h] * o[h], axis=-1, keepdims=True) + 1e-6)
        o_ref[:, hs] = (on * nw * _silu(zz[:, hs])).astype(o_ref.dtype)


def _gdn(proj_g, tail, conv_w, a_log, dt_bias, norm_w):
    T = proj_g.shape[0]
    C = GDN_CHUNK
    pad = lambda t: jnp.zeros((1, LANES), F32).at[0, :GDN_HEADS].set(t)
    ab_block = tail.shape[1] // LANES - 1
    return pl.pallas_call(
        _gdn_kernel,
        grid=(T // C,),
        in_specs=[pl.BlockSpec((C, 3 * GDN_W), lambda i: (i, 0)),
                  pl.BlockSpec((C, GDN_W), lambda i: (i, 3)),
                  pl.BlockSpec((C, LANES), lambda i: (i, ab_block)),
                  pl.BlockSpec((CONV_W, 3 * GDN_W), lambda i: (0, 0)),
                  pl.BlockSpec((1, LANES), lambda i: (0, 0)),
                  pl.BlockSpec((1, LANES), lambda i: (0, 0)),
                  pl.BlockSpec((1, HEAD_DIM), lambda i: (0, 0))],
        out_specs=pl.BlockSpec((C, GDN_W), lambda i: (i, 0)),
        out_shape=jax.ShapeDtypeStruct((T, GDN_W), BF16),
        scratch_shapes=[pltpu.VMEM((SUBLANES, 3 * GDN_W), F32),
                        pltpu.VMEM((GDN_HEADS, HEAD_DIM, HEAD_DIM), F32)],
        compiler_params=_cparams(("arbitrary",)),
        name="gdn",
    )(proj_g, proj_g, tail, conv_w, pad(a_log), pad(dt_bias), norm_w.reshape(1, HEAD_DIM))


def _gelu(x):
    return 0.5 * x * (1.0 + lax.erf(x * (2.0 ** -0.5)))


def _sg_kernel(u_ref, v_ref, nw_ref, nb_ref, w_ref, b_ref, o_ref):
    C = SG_CHUNK
    causal = _iota((C, C), 0) >= _iota((C, C), 1)
    for g in range(SG_GROUPS):
        gs = slice(g * HEAD_DIM, (g + 1) * HEAD_DIM)
        u = _gelu(u_ref[:, gs])
        v = _gelu(v_ref[:, gs])
        mu = jnp.mean(v, axis=-1, keepdims=True)
        var = jnp.mean(jnp.square(v - mu), axis=-1, keepdims=True)
        vn = (v - mu) * lax.rsqrt(var + LN_EPS) * nw_ref[:, gs] + nb_ref[:, gs]
        ws = jnp.where(causal, w_ref[g], 0.0)
        mixed = jnp.dot(ws, vn, preferred_element_type=F32) + b_ref[:, g:g + 1]
        o_ref[:, gs] = (u * mixed).astype(o_ref.dtype)


def _spatial_gating(tail, norm_w, norm_b, sg_w, sg_b):
    T = tail.shape[0]
    C = SG_CHUNK
    return pl.pallas_call(
        _sg_kernel,
        grid=(T // C,),
        in_specs=[pl.BlockSpec((C, SG_W), lambda i: (i, 0)),
                  pl.BlockSpec((C, SG_W), lambda i: (i, 1)),
                  pl.BlockSpec((1, SG_W), lambda i: (0, 0)),
                  pl.BlockSpec((1, SG_W), lambda i: (0, 0)),
                  pl.BlockSpec((SG_GROUPS, C, C), lambda i: (0, 0, 0)),
                  pl.BlockSpec((C, SG_GROUPS), lambda i: (0, 0))],
        out_specs=pl.BlockSpec((C, SG_W), lambda i: (i, 0)),
        out_shape=jax.ShapeDtypeStruct((T, SG_W), BF16),
        compiler_params=_cparams(("arbitrary",)),
        name="spatial_gating",
    )(tail, tail, norm_w.reshape(1, SG_W), norm_b.reshape(1, SG_W), sg_w, sg_b.T)


def _layer_norm_rows(r, w, b):
    mu = jnp.mean(r, axis=-1, keepdims=True)
    d = r - mu
    var = jnp.mean(d * d, axis=-1, keepdims=True)
    return d * lax.rsqrt(var + LN_EPS) * w + b


def _outproj_kernel(x_ref, ysb_ref, ygdn_ref, ysg_ref, w_ref, gt_ref, lw_ref, lb_ref,
                    sc_ref, sh_ref, xo_ref, ho_ref, *, alpha):
    y = jnp.dot(ysb_ref[...], w_ref[0:SB_W, :], preferred_element_type=F32)
    y = y + jnp.dot(ygdn_ref[...], w_ref[SB_W:SB_W + GDN_W, :], preferred_element_type=F32)
    y = y + jnp.dot(ysg_ref[...], w_ref[SB_W + GDN_W:, :], preferred_element_type=F32)
    r = alpha * x_ref[...] + (1.0 + gt_ref[...]) * y
    xn = _layer_norm_rows(r, lw_ref[...], lb_ref[...])
    xo_ref[...] = xn
    ho_ref[...] = xn * (1.0 + sc_ref[...]) + sh_ref[...]


def _outproj_ln(x, y_sb, y_gdn, y_sg, w_out_bf16, mod, ln_w, ln_b, alpha):
    T, D = x.shape
    tm = 256
    row = lambda idx: pl.BlockSpec((1, D), lambda i: (0, idx))
    return pl.pallas_call(
        functools.partial(_outproj_kernel, alpha=alpha),
        grid=(T // tm,),
        in_specs=[pl.BlockSpec((tm, D), lambda i: (i, 0)),
                  pl.BlockSpec((tm, SB_W), lambda i: (i, 0)),
                  pl.BlockSpec((tm, GDN_W), lambda i: (i, 0)),
                  pl.BlockSpec((tm, SG_W), lambda i: (i, 0)),
                  pl.BlockSpec((D, D), lambda i: (0, 0)),
                  row(2), row(0), row(0), row(4), row(3)],
        out_specs=[pl.BlockSpec((tm, D), lambda i: (i, 0)),
                   pl.BlockSpec((tm, D), lambda i: (i, 0))],
        out_shape=[jax.ShapeDtypeStruct((T, D), F32), jax.ShapeDtypeStruct((T, D), F32)],
        compiler_params=_cparams(("arbitrary",)),
        name="outproj_ln",
    )(x, y_sb, y_gdn, y_sg, w_out_bf16, mod, ln_w.reshape(1, D), ln_b.reshape(1, D), mod, mod)


def _first_max(vals, idx, axis):
    m = jnp.max(vals, axis=axis, keepdims=True)
    big = jnp.iinfo(jnp.int32).max
    first = jnp.min(jnp.where(vals == m, idx, big), axis=axis, keepdims=True)
    return m, first


def _router_kernel(h_ref, wt_ref, b_ref, idx_ref, wgt_ref, rank_ref, cnt_ref, carry_ref):
    E, G, S = N_EXPERTS, N_GROUPS, GROUP_SIZE
    tm = h_ref.shape[0]
    step = pl.program_id(0)

    @pl.when(step == 0)
    def _():
        carry_ref[...] = jnp.zeros_like(carry_ref)

    h = h_ref[...]
    h_hi = h.astype(BF16)
    h_lo = (h - h_hi.astype(F32)).astype(BF16)
    wt = wt_ref[...]
    w_hi = wt.astype(BF16)
    w_lo = (wt - w_hi.astype(F32)).astype(BF16)
    nt = (((1,), (1,)), ((), ()))
    logits = (lax.dot_general(w_hi, h_hi, nt, preferred_element_type=F32)
              + lax.dot_general(w_hi, h_lo, nt, preferred_element_type=F32)
              + lax.dot_general(w_lo, h_hi, nt, preferred_element_type=F32))
    scores = _sigmoid(logits)
    sel = scores + b_ref[:, 0:1]

    sel3 = sel.reshape(G, S, tm)
    e_in_g = _iota((G, S, tm), 1)
    m1, i1 = _first_max(sel3, e_in_g, 1)
    m2 = jnp.max(jnp.where(e_in_g == i1, -jnp.inf, sel3), axis=1, keepdims=True)
    gs = m1 + m2

    g_idx = _iota((G, 1, tm), 0)
    gmask = jnp.zeros((G, 1, tm), jnp.bool_)
    for _ in range(TOPK_GROUPS):
        _, gi = _first_max(gs, g_idx, 0)
        hit = g_idx == gi
        gmask = jnp.logical_or(gmask, hit)
        gs = jnp.where(hit, -jnp.inf, gs)
    cur = jnp.where(gmask, sel3, -jnp.inf).reshape(E, tm)

    e_idx = _iota((E, tm), 0)
    chosen = jnp.zeros((E, tm), jnp.bool_)
    idx_rows = []
    score_rows = []
    for _ in range(TOP_K):
        _, ei = _first_max(cur, e_idx, 0)
        hit = e_idx == ei
        chosen = jnp.logical_or(chosen, hit)
        cur = jnp.where(hit, -jnp.inf, cur)
        idx_rows.append(ei)
        score_rows.append(jnp.sum(jnp.where(hit, scores, 0.0), axis=0, keepdims=True))
    top_idx = jnp.concatenate(idx_rows, axis=0)
    top_sc = jnp.concatenate(score_rows, axis=0)
    idx_ref[...] = top_idx
    wgt_ref[...] = top_sc / jnp.sum(top_sc, axis=0, keepdims=True) * ROUTED_SCALE

    chosen_b = chosen.astype(BF16)
    before = (_iota((tm, tm), 0) < _iota((tm, tm), 1)).astype(BF16)
    excl = jnp.dot(chosen_b, before, preferred_element_type=F32) + carry_ref[:, 0:1]
    rank_rows = [jnp.sum(jnp.where(e_idx == idx_rows[k], excl, 0.0), axis=0, keepdims=True)
                 for k in range(TOP_K)]
    rank_ref[...] = jnp.concatenate(rank_rows, axis=0).astype(I32)
    carry_ref[...] = carry_ref[...] + jnp.sum(chosen.astype(F32), axis=1, keepdims=True)
    cnt_ref[...] = carry_ref[...]


def _router(h2, w_router_t, router_bias):
    T, D = h2.shape
    tm = ROUTER_TM
    E, K = N_EXPERTS, TOP_K
    bias = jnp.broadcast_to(router_bias.reshape(E, 1), (E, LANES))
    tok = lambda: pl.BlockSpec((K, tm), lambda i: (0, i))
    return pl.pallas_call(
        _router_kernel,
        grid=(T // tm,),
        in_specs=[pl.BlockSpec((tm, D), lambda i: (i, 0)),
                  pl.BlockSpec((E, D), lambda i: (0, 0)),
                  pl.BlockSpec((E, LANES), lambda i: (0, 0))],
        out_specs=[tok(), tok(), tok(), pl.BlockSpec((E, LANES), lambda i: (0, 0))],
        out_shape=[jax.ShapeDtypeStruct((K, T), I32), jax.ShapeDtypeStruct((K, T), F32),
                   jax.ShapeDtypeStruct((K, T), I32), jax.ShapeDtypeStruct((E, LANES), F32)],
        scratch_shapes=[pltpu.VMEM((E, LANES), F32)],
        compiler_params=_cparams(("arbitrary",)),
        name="router",
    )(h2, w_router_t, bias)


def _plan_kernel(cnt_ref, idx_ref, rank_ref, pos_ref, texp_ref, end_ref, *, tile, n_tiles):
    E = N_EXPERTS
    tm = idx_ref.shape[1]
    cnt = cnt_ref[...]
    ntile = jnp.floor((cnt + (tile - 1)) * (1.0 / tile))
    lower = (_iota((E, E), 0) >= _iota((E, E), 1)).astype(F32)
    tile_end = jnp.dot(lower, ntile, preferred_element_type=F32)
    row_start = (tile_end - ntile) * tile
    e_idx = _iota((E, tm), 0)
    start_col = row_start[:, 0:1]
    for k in range(TOP_K):
        base = jnp.sum(jnp.where(e_idx == idx_ref[k:k + 1, :], start_col, 0.0), axis=0, keepdims=True)
        pos_ref[k:k + 1, :] = base.astype(I32) + rank_ref[k:k + 1, :]
    tile_i = _iota((E, n_tiles), 1).astype(F32)
    texp = jnp.sum((tile_end[:, 0:1] <= tile_i).astype(F32), axis=0, keepdims=True)
    texp_ref[...] = jnp.minimum(texp, E - 1.0).astype(I32)
    end_ref[...] = (tile_end * tile).astype(I32)


def _plan(cnt, idx_t, rank_t, tile, n_tiles):
    K, T = idx_t.shape
    E = N_EXPERTS
    tm = min(2048, T)
    tok = lambda: pl.BlockSpec((K, tm), lambda i: (0, i))
    return pl.pallas_call(
        functools.partial(_plan_kernel, tile=tile, n_tiles=n_tiles),
        grid=(T // tm,),
        in_specs=[pl.BlockSpec((E, LANES), lambda i: (0, 0)), tok(), tok()],
        out_specs=[tok(), pl.BlockSpec((1, n_tiles), lambda i: (0, 0)),
                   pl.BlockSpec((E, LANES), lambda i: (0, 0))],
        out_shape=[jax.ShapeDtypeStruct((K, T), I32), jax.ShapeDtypeStruct((1, n_tiles), I32),
                   jax.ShapeDtypeStruct((E, LANES), I32)],
        compiler_params=_cparams(("arbitrary",)),
        name="route_plan",
    )(cnt, idx_t, rank_t)


def _invert_kernel(pos_ref, init_ref, inv_ref, sem, *, n_pairs):
    fill = pltpu.make_async_copy(init_ref, inv_ref, sem)
    fill.start()
    fill.wait()

    def scatter(q, c):
        inv_ref[pos_ref[q]] = q
        return c

    lax.fori_loop(0, n_pairs, scatter, 0, unroll=8)


def _invert(pos_flat, inv_init):
    n_pairs = pos_flat.shape[0]
    return pl.pallas_call(
        functools.partial(_invert_kernel, n_pairs=n_pairs),
        in_specs=[pl.BlockSpec(memory_space=pltpu.SMEM), pl.BlockSpec(memory_space=pl.ANY)],
        out_specs=pl.BlockSpec(memory_space=pltpu.SMEM),
        out_shape=jax.ShapeDtypeStruct(inv_init.shape, I32),
        scratch_shapes=[pltpu.SemaphoreType.DMA(())],
        name="route_invert",
    )(pos_flat, inv_init)


def _expert_kernel(texp_ref, used_ref, inv_ref, h_ref, w1_ref, w3_ref, w2_ref, ys_ref,
                   xbuf, obuf, gsem, ssem, *, tm, n_tok, n_tiles):
    i = pl.program_id(0)
    used = used_ref[0]
    slot = lax.rem(i, 2)
    other = 1 - slot
    D = xbuf.shape[2]
    F = w1_ref.shape[2]

    def gather_start(tile_idx, s, rows):
        for j in rows:
            tok = inv_ref[tile_idx * tm + j] & (n_tok - 1)
            pltpu.make_async_copy(h_ref.at[pl.ds(tok, 1)], xbuf.at[s, pl.ds(j, 1)], gsem.at[s]).start()

    def gather_wait(s):
        pltpu.make_async_copy(h_ref.at[pl.ds(0, tm)], xbuf.at[s], gsem.at[s]).wait()

    def scatter_start(tile_idx, s, rows):
        for j in rows:
            dst = inv_ref[tile_idx * tm + j]
            pltpu.make_async_copy(obuf.at[s, pl.ds(j, 1)], ys_ref.at[pl.ds(dst, 1)], ssem.at[s]).start()

    def scatter_wait(s):
        pltpu.make_async_copy(obuf.at[s], ys_ref.at[pl.ds(0, tm)], ssem.at[s]).wait()

    @pl.when(i == 0)
    def _():
        obuf[1] = jnp.zeros((tm, D), obuf.dtype)
        gather_start(0, 0, range(tm))

    @pl.when(jnp.logical_and(i >= 1, i < used))
    def _():
        scatter_wait(slot)

    @pl.when(i < used)
    def _():
        gather_wait(slot)
        nxt = jnp.minimum(i + 1, used - 1)
        prv = jnp.where(i == 0, n_tiles, i - 1)
        chunks = EXPERT_K_CHUNKS
        per = tm // (2 * chunks)
        kc = D // chunks

        def dma_group(gi):
            rows = range(gi * per, (gi + 1) * per)
            gather_start(nxt, other, rows)
            scatter_start(prv, other, rows)

        h1 = jnp.zeros((tm, F), F32)
        h3 = jnp.zeros((tm, F), F32)
        for c in range(chunks):
            dma_group(c)
            xc = xbuf[slot, :, c * kc:(c + 1) * kc].astype(BF16)
            h1 = h1 + jnp.dot(xc, w1_ref[0, c * kc:(c + 1) * kc, :].astype(BF16), preferred_element_type=F32)
            h3 = h3 + jnp.dot(xc, w3_ref[0, c * kc:(c + 1) * kc, :].astype(BF16), preferred_element_type=F32)
        a = (_silu(h1) * h3).astype(BF16)
        for c in range(chunks):
            dma_group(chunks + c)
            obuf[slot, :, c * kc:(c + 1) * kc] = jnp.dot(
                a, w2_ref[0, :, c * kc:(c + 1) * kc].astype(BF16), preferred_element_type=F32)

    @pl.when(i == used - 1)
    def _():
        scatter_wait(other)
        scatter_start(i, slot, range(tm))
        gather_wait(other)
        scatter_wait(slot)


def _experts(tile_expert, tiles_used, inv, h2, w1, w3, w2, layer):
    T, D = h2.shape
    F = w1.shape[-1]
    tm = EXPERT_TM
    assert T & (T - 1) == 0
    n_tiles = inv.shape[0] // tm - 1
    wspec = lambda a, b: pl.BlockSpec((None, 1, a, b), lambda i, te, nu, iv: (layer, te[i], 0, 0))
    return pl.pallas_call(
        functools.partial(_expert_kernel, tm=tm, n_tok=T, n_tiles=n_tiles),
        grid_spec=pltpu.PrefetchScalarGridSpec(
            num_scalar_prefetch=3,
            grid=(n_tiles,),
            in_specs=[pl.BlockSpec(memory_space=pl.ANY), wspec(D, F), wspec(D, F), wspec(F, D)],
            out_specs=pl.BlockSpec(memory_space=pl.ANY),
            scratch_shapes=[pltpu.VMEM((2, tm, D), F32), pltpu.VMEM((2, tm, D), F32),
                            pltpu.SemaphoreType.DMA((2,)), pltpu.SemaphoreType.DMA((2,))]),
        out_shape=jax.ShapeDtypeStruct((T * TOP_K + tm, D), F32),
        compiler_params=_cparams(("arbitrary",)),
        name="moe_experts",
    )(tile_expert, tiles_used, inv, h2, w1, w3, w2)


def _combine_kernel(*refs, alpha):
    y_refs = refs[:TOP_K]
    (x_ref, h_ref, g_ref, s1_ref, s3_ref, s2_ref, gt_ref, lw_ref, lb_ref, sc_ref, sh_ref,
     xo_ref, ho_ref) = refs[TOP_K:]
    hb = h_ref[...].astype(BF16)
    a = _silu(jnp.dot(hb, s1_ref[...], preferred_element_type=F32))
    a = a * jnp.dot(hb, s3_ref[...], preferred_element_type=F32)
    y = jnp.dot(a.astype(BF16), s2_ref[...], preferred_element_type=F32)
    g = g_ref[...]
    for k in range(TOP_K):
        y = y + g[:, k:k + 1] * y_refs[k][...]
    r = alpha * x_ref[...] + (1.0 + gt_ref[...]) * y
    xn = _layer_norm_rows(r, lw_ref[...], lb_ref[...])
    xo_ref[...] = xn
    ho_ref[...] = (xn * (1.0 + sc_ref[...]) + sh_ref[...]).astype(ho_ref.dtype)


def _combine(ys, x, h2, gates, s1, s3, s2, mod, ln_w, ln_b, mod_next, alpha):
    T, D = x.shape
    F = s1.shape[-1]
    tc = COMBINE_TC
    nb = T // tc
    row = lambda idx: pl.BlockSpec((1, D), lambda i: (0, idx))
    full = lambda a, b: pl.BlockSpec((a, b), lambda i: (0, 0))
    tok = lambda: pl.BlockSpec((tc, D), lambda i: (i, 0))
    pair = lambda k: pl.BlockSpec((tc, D), lambda i: (k * nb + i, 0))
    return pl.pallas_call(
        functools.partial(_combine_kernel, alpha=alpha),
        grid=(nb,),
        in_specs=[pair(k) for k in range(TOP_K)] + [
            tok(), tok(), pl.BlockSpec((tc, TOP_K), lambda i: (i, 0)),
            full(D, F), full(D, F), full(F, D), row(5), row(0), row(0), row(1), row(0)],
        out_specs=[tok(), tok()],
        out_shape=[jax.ShapeDtypeStruct((T, D), F32), jax.ShapeDtypeStruct((T, D), BF16)],
        compiler_params=_cparams(("arbitrary",)),
        name="moe_combine",
    )(*([ys] * TOP_K), x, h2, gates, s1, s3, s2, mod, ln_w.reshape(1, D), ln_b.reshape(1, D),
      mod_next, mod_next)


def kernel(x, c, w_ada, b_ada, w_in, conv_w, gdn_a_log, gdn_dt_bias, gdn_norm_w, sg_norm_w, sg_norm_b, sg_w, sg_b, w_out, ln1_w, ln1_b, w_router, router_bias, exp_w1, exp_w3, exp_w2, sh_w1, sh_w3, sh_w2, ln2_w, ln2_b):
    B, T, D = x.shape
    assert B == 1
    L = w_ada.shape[0]
    alpha = (2.0 * L) ** 0.25
    main_cols = 3 * SB_W + 4 * GDN_W
    assert main_cols % 512 == 0 and w_in.shape[2] == main_cols + 2 * GDN_HEADS + 2 * SG_W
    n_tiles = (T * TOP_K) // EXPERT_TM + N_EXPERTS
    n_rows = n_tiles * EXPERT_TM
    n_tiles_pad = -(-n_tiles // LANES) * LANES

    xt = x.reshape(T, D)
    mod_all = _ada_mod(c, w_ada, b_ada)

    ab_cols = w_in[:, :, main_cols:main_cols + 2 * GDN_HEADS]
    ab_cols = jnp.pad(ab_cols, ((0, 0), (0, 0), (0, LANES - 2 * GDN_HEADS)))
    w_tail = jnp.concatenate([w_in[:, :, main_cols + 2 * GDN_HEADS:], ab_cols], axis=-1)
    w_out_b = w_out.astype(BF16)
    w_router_t = jnp.swapaxes(w_router, 1, 2)
    s1, s3, s2 = sh_w1.astype(BF16), sh_w3.astype(BF16), sh_w2.astype(BF16)

    inv_init = T * TOP_K + lax.rem(jnp.arange(n_rows + EXPERT_TM, dtype=I32), EXPERT_TM)

    h = _modulate(xt, mod_all[0], 1, 0)
    for l in range(L):
        mod = mod_all[l]
        mod_next = mod_all[min(l + 1, L - 1)]
        sb_qkv = _proj(h, w_in, l, 0, 3 * SB_W // 512, 512, BF16)
        proj_g = _proj(h, w_in, l, 3 * SB_W // 512, 4 * GDN_W // 512, 512, F32)
        tail = _proj(h, w_tail, l, 0, 1, w_tail.shape[2], F32)
        y_sb = _sb_attention(sb_qkv)
        y_gdn = _gdn(proj_g, tail, conv_w[l], gdn_a_log[l], gdn_dt_bias[l], gdn_norm_w[l])
        y_sg = _spatial_gating(tail, sg_norm_w[l], sg_norm_b[l], sg_w[l], sg_b[l])
        xt, h2 = _outproj_ln(xt, y_sb, y_gdn, y_sg, w_out_b[l], mod, ln1_w[l], ln1_b[l], alpha)
        idx_t, wgt_t, rank_t, cnt = _router(h2, w_router_t[l], router_bias[l])
        pos_t, tile_expert, seg_end = _plan(cnt, idx_t, rank_t, EXPERT_TM, n_tiles_pad)
        inv = _invert(pos_t.reshape(-1), inv_init)
        tiles_used = seg_end[N_EXPERTS - 1:, 0] // EXPERT_TM
        ys = _experts(tile_expert.reshape(-1), tiles_used, inv, h2, exp_w1, exp_w3, exp_w2, l)
        xt, h = _combine(ys, xt, h2, wgt_t.T, s1[l], s3[l], s2[l], mod, ln2_w[l], ln2_b[l],
                         mod_next, alpha)
    return xt.reshape(B, T, D)
```

```python
import functools

import jax
import jax.numpy as jnp
from jax import lax
from jax.experimental import pallas as pl
from jax.experimental.pallas import tpu as pltpu

F32 = jnp.float32
BF16 = jnp.bfloat16
I32 = jnp.int32
HI16 = -65536

LANES = 128
SUBLANES = 8
VMEM_LIMIT = 56 * 1024 * 1024

HEAD_DIM = 128
SB_HEADS = 4
GDN_HEADS = 8
SG_GROUPS = 4
SB_W = SB_HEADS * HEAD_DIM
GDN_W = GDN_HEADS * HEAD_DIM
SG_W = SG_GROUPS * HEAD_DIM
GDN_CHUNK = 64
CONV_W = 4
SG_CHUNK = 128
N_EXPERTS = 64
TOP_K = 8
N_GROUPS = 8
GROUP_SIZE = N_EXPERTS // N_GROUPS
TOPK_GROUPS = 4
ROUTED_SCALE = 2.5
LN_EPS = 1e-5

SB_TQ = 256
SB_TK = 256
SB_UNDERFLOW = 110.0
EXPERT_TM = 256
ROUTER_TM = 512
EXPERT_K_CHUNKS = 8
COMBINE_TC = 128


def _cparams(sem):
    return pltpu.CompilerParams(dimension_semantics=sem, vmem_limit_bytes=VMEM_LIMIT)


def _sigmoid(x):
    return 1.0 / (1.0 + jnp.exp(-x))


def _silu(x):
    return x * _sigmoid(x)


def _softplus(x):
    return jnp.maximum(x, 0.0) + jnp.log(1.0 + jnp.exp(-jnp.abs(x)))


def _iota(shape, dim):
    return lax.broadcasted_iota(I32, shape, dim)


def _pack_bf16_pair(lo, hi):
    lo_bits = lax.bitcast_convert_type(lo.astype(BF16).astype(F32), I32)
    hi_bits = lax.bitcast_convert_type(hi.astype(BF16).astype(F32), I32)
    return jnp.bitwise_or(jnp.bitwise_and(hi_bits, HI16), lax.shift_right_logical(lo_bits, 16))


def _unpack_bf16_pair(words):
    lo = lax.bitcast_convert_type(jnp.left_shift(words, 16), F32)
    hi = lax.bitcast_convert_type(jnp.bitwise_and(words, HI16), F32)
    return lo, hi


def _ada_kernel(c_ref, w_ref, b_ref, o_ref):
    cb = _silu(c_ref[...])
    tn = w_ref.shape[2]
    for j in range(tn // LANES):
        sl = slice(j * LANES, (j + 1) * LANES)
        o_ref[0, :, sl] = jnp.sum(w_ref[0, :, sl] * cb, axis=0, keepdims=True) + b_ref[0, :, sl]


def _ada_mod(c, w_ada, b_ada):
    L, D, N = w_ada.shape
    tn = 1536
    cb = jnp.broadcast_to(c.reshape(D, 1), (D, LANES))
    return pl.pallas_call(
        _ada_kernel,
        grid=(L, N // tn),
        in_specs=[pl.BlockSpec((D, LANES), lambda l, j: (0, 0)),
                  pl.BlockSpec((1, D, tn), lambda l, j: (l, 0, j)),
                  pl.BlockSpec((1, 1, tn), lambda l, j: (l, 0, j))],
        out_specs=pl.BlockSpec((1, 1, tn), lambda l, j: (l, 0, j)),
        out_shape=jax.ShapeDtypeStruct((L, 1, N), F32),
        compiler_params=_cparams(("arbitrary", "arbitrary")),
        name="ada_mod",
    )(cb, w_ada, b_ada.reshape(L, 1, N))


def _modulate_kernel(x_ref, sc_ref, sh_ref, o_ref):
    o_ref[...] = (x_ref[...] * (1.0 + sc_ref[...]) + sh_ref[...]).astype(o_ref.dtype)


def _modulate(x, mod, sc_idx, sh_idx):
    T, D = x.shape
    tm = 512
    return pl.pallas_call(
        _modulate_kernel,
        grid=(T // tm,),
        in_specs=[pl.BlockSpec((tm, D), lambda i: (i, 0)),
                  pl.BlockSpec((1, D), lambda i: (0, sc_idx)),
                  pl.BlockSpec((1, D), lambda i: (0, sh_idx))],
        out_specs=pl.BlockSpec((tm, D), lambda i: (i, 0)),
        out_shape=jax.ShapeDtypeStruct((T, D), BF16),
        compiler_params=_cparams(("arbitrary",)),
        name="modulate",
    )(x, mod, mod)


def _proj_kernel(x_ref, w_ref, o_ref, wb_ref):
    @pl.when(pl.program_id(1) == 0)
    def _():
        wb_ref[...] = w_ref[0].astype(BF16)

    o_ref[...] = jnp.dot(x_ref[...], wb_ref[...], preferred_element_type=F32).astype(o_ref.dtype)


def _proj(h, w, layer, col_block0, n_blocks, tn, out_dtype):
    T, K = h.shape
    tm = min(1024, T)
    return pl.pallas_call(
        _proj_kernel,
        grid=(n_blocks, T // tm),
        in_specs=[pl.BlockSpec((tm, K), lambda j, i: (i, 0)),
                  pl.BlockSpec((1, K, tn), lambda j, i: (layer, 0, j + col_block0))],
        out_specs=pl.BlockSpec((tm, tn), lambda j, i: (i, j)),
        out_shape=jax.ShapeDtypeStruct((T, n_blocks * tn), out_dtype),
        scratch_shapes=[pltpu.VMEM((K, tn), BF16)],
        compiler_params=_cparams(("arbitrary", "arbitrary")),
        name="in_proj",
    )(h, w)


def _sb_kernel(q_ref, k_ref, v_ref, o_ref, *, tq, tk, scale):
    i = pl.program_id(1)
    q = q_ref[...]
    tri = (_iota((tk, tk), 0) >= _iota((tk, tk), 1)).astype(BF16)

    def block(j, r, diagonal):
        ks = pl.multiple_of(j * tk, tk)
        k = k_ref[pl.ds(ks, tk), :]
        v = v_ref[pl.ds(ks, tk), :]
        z = lax.dot_general(q, k, (((1,), (1,)), ((), ())), preferred_element_type=F32) * scale
        sp = _softplus(z)
        if diagonal:
            causal = _iota((tq, tk), 1) < _iota((tq, tk), 0)
            sp = jnp.where(causal, sp, 0.0)
        sp_hi = sp.astype(BF16)
        sp_lo = (sp - sp_hi.astype(F32)).astype(BF16)
        c = (jnp.dot(sp_hi, tri, preferred_element_type=F32)
             + jnp.dot(sp_lo, tri, preferred_element_type=F32))
        a = jnp.exp(z - c - r)
        if diagonal:
            a = jnp.where(causal, a, 0.0)
        pv = jnp.dot(a.astype(BF16), v, preferred_element_type=F32)
        return pv, r + c[:, 0:1]

    acc, r = block(i, jnp.zeros((tq, 1), F32), True)

    def cond(carry):
        step, _, _, rmin = carry
        return jnp.logical_and(step <= i, rmin < SB_UNDERFLOW)

    def body(carry):
        step, acc, r, _ = carry
        pv, r = block(i - step, r, False)
        return step + 1, acc + pv, r, jnp.min(r)

    _, acc, _, _ = lax.while_loop(cond, body, (jnp.int32(1), acc, r, jnp.min(r)))
    o_ref[...] = acc.astype(o_ref.dtype)


def _sb_attention(qkv):
    T = qkv.shape[0]
    tq, tk = SB_TQ, SB_TK
    assert tq == tk
    kern = functools.partial(_sb_kernel, tq=tq, tk=tk, scale=HEAD_DIM ** -0.5)
    return pl.pallas_call(
        kern,
        grid=(SB_HEADS, T // tq),
        in_specs=[pl.BlockSpec((tq, HEAD_DIM), lambda h, i: (i, h)),
                  pl.BlockSpec((T, HEAD_DIM), lambda h, i: (0, SB_HEADS + h)),
                  pl.BlockSpec((T, HEAD_DIM), lambda h, i: (0, 2 * SB_HEADS + h))],
        out_specs=pl.BlockSpec((tq, HEAD_DIM), lambda h, i: (i, h)),
        out_shape=jax.ShapeDtypeStruct((T, SB_W), BF16),
        compiler_params=_cparams(("arbitrary", "arbitrary")),
        name="sb_attention",
    )(qkv, qkv, qkv)


def _gdn_kernel(qkv_ref, z_ref, ab_ref, cw_ref, alog_ref, dtb_ref, nw_ref, o_ref,
                tail_ref, state_ref):
    C = GDN_CHUNK
    Dh = HEAD_DIM
    step = pl.program_id(0)

    @pl.when(step == 0)
    def _():
        tail_ref[...] = jnp.zeros_like(tail_ref)
        state_ref[...] = jnp.zeros_like(state_ref)

    x = qkv_ref[...]
    ext = jnp.concatenate([tail_ref[...], x], axis=0)
    cw = cw_ref[...]
    conv = cw[CONV_W - 1:CONV_W, :] * x
    for s in range(1, CONV_W):
        conv = conv + cw[CONV_W - 1 - s:CONV_W - s, :] * ext[SUBLANES - s:SUBLANES - s + C, :]
    tail_ref[...] = x[C - SUBLANES:, :]
    act = _silu(conv)

    ab = ab_ref[...]
    g = -jnp.exp(alog_ref[...]) * _softplus(ab + dtb_ref[...])
    beta_all = _sigmoid(ab)
    ri = _iota((C, C), 0)
    ci = _iota((C, C), 1)
    tril = (ri >= ci)
    strict = (ri > ci)
    gc = jnp.dot(tril.astype(F32), g, preferred_element_type=F32,
                 precision=lax.Precision.HIGHEST)
    gc_t = jnp.transpose(gc)
    eye = (ri == ci).astype(F32)
    nw = nw_ref[...]
    zz = z_ref[...]

    H = range(GDN_HEADS)
    nt = (((1,), (1,)), ((), ()))
    mm = lambda a, b: jnp.dot(a, b, preferred_element_type=F32)
    qs = [act[:, h * Dh:(h + 1) * Dh] for h in H]
    ks = [act[:, GDN_W + h * Dh:GDN_W + (h + 1) * Dh] for h in H]
    vs = [act[:, 2 * GDN_W + h * Dh:2 * GDN_W + (h + 1) * Dh] for h in H]
    qs = [q * lax.rsqrt(jnp.sum(q * q, axis=-1, keepdims=True) + 1e-6) * (Dh ** -0.5) for q in qs]
    ks = [k * lax.rsqrt(jnp.sum(k * k, axis=-1, keepdims=True) + 1e-6) for k in ks]
    beta = [beta_all[:, SUBLANES + h:SUBLANES + h + 1] for h in H]
    gcol = [gc[:, h:h + 1] for h in H]
    grow = [gc_t[h:h + 1, :] for h in H]
    glast = [gc[C - 1:C, h:h + 1] for h in H]
    decay = [jnp.exp(jnp.where(tril, gcol[h] - grow[h], -jnp.inf)) for h in H]
    eg = [jnp.exp(gcol[h]) for h in H]
    kb = [ks[h] * beta[h] for h in H]
    kk = [lax.dot_general(kb[h], ks[h], nt, preferred_element_type=F32) for h in H]
    xm = [-jnp.where(strict, kk[h] * decay[h], 0.0) for h in H]
    tm = [eye + xm[h] for h in H]
    p = xm
    for _ in range(5):
        p = [mm(p[h], p[h]) for h in H]
        tm = [tm[h] + mm(tm[h], p[h]) for h in H]
    sol = [mm(tm[h], jnp.concatenate([vs[h] * beta[h], kb[h] * eg[h]], axis=1)) for h in H]
    qk = [lax.dot_general(qs[h], ks[h], nt, preferred_element_type=F32) for h in H]
    qk = [jnp.where(tril, qk[h] * decay[h], 0.0) for h in H]
    k_dec = [ks[h] * jnp.exp(glast[h] - gcol[h]) for h in H]
    state = [state_ref[h] for h in H]
    ws = [mm(jnp.concatenate([sol[h][:, Dh:], qs[h] * eg[h]], axis=0), state[h]) for h in H]
    v_new = [sol[h][:, :Dh] - ws[h][:C] for h in H]
    o = [ws[h][C:] + mm(qk[h], v_new[h]) for h in H]
    for h in H:
        state_ref[h] = state[h] * jnp.exp(glast[h]) + lax.dot_general(
            k_dec[h], v_new[h], (((0,), (0,)), ((), ())), preferred_element_type=F32)
    for h in H:
        hs = slice(h * Dh, (h + 1) * Dh)
        on = o[h] * lax.rsqrt(jnp.mean(o[h] * o[h], axis=-1, keepdims=True) + 1e-6)
        o_ref[:, hs] = (on * nw * _silu(zz[:, hs])).astype(o_ref.dtype)


def _gdn(proj_g, tail, conv_w, a_log, dt_bias, norm_w):
    T = proj_g.shape[0]
    C = GDN_CHUNK
    pad = lambda t: jnp.zeros((1, LANES), F32).at[0, :GDN_HEADS].set(t)
    ab_block = tail.shape[1] // LANES - 1
    return pl.pallas_call(
        _gdn_kernel,
        grid=(T // C,),
        in_specs=[pl.BlockSpec((C, 3 * GDN_W), lambda i: (i, 0)),
                  pl.BlockSpec((C, GDN_W), lambda i: (i, 3)),
                  pl.BlockSpec((C, LANES), lambda i: (i, ab_block)),
                  pl.BlockSpec((CONV_W, 3 * GDN_W), lambda i: (0, 0)),
                  pl.BlockSpec((1, LANES), lambda i: (0, 0)),
                  pl.BlockSpec((1, LANES), lambda i: (0, 0)),
                  pl.BlockSpec((1, HEAD_DIM), lambda i: (0, 0))],
        out_specs=pl.BlockSpec((C, GDN_W), lambda i: (i, 0)),
        out_shape=jax.ShapeDtypeStruct((T, GDN_W), BF16),
        scratch_shapes=[pltpu.VMEM((SUBLANES, 3 * GDN_W), F32),
                        pltpu.VMEM((GDN_HEADS, HEAD_DIM, HEAD_DIM), F32)],
        compiler_params=_cparams(("arbitrary",)),
        name="gdn",
    )(proj_g, proj_g, tail, conv_w, pad(a_log), pad(dt_bias), norm_w.reshape(1, HEAD_DIM))


def _gelu(x):
    return 0.5 * x * (1.0 + lax.erf(x * (2.0 ** -0.5)))


def _sg_kernel(u_ref, v_ref, nw_ref, nb_ref, w_ref, b_ref, o_ref):
    C = SG_CHUNK
    causal = _iota((C, C), 0) >= _iota((C, C), 1)
    for g in range(SG_GROUPS):
        gs = slice(g * HEAD_DIM, (g + 1) * HEAD_DIM)
        u = _gelu(u_ref[:, gs])
        v = _gelu(v_ref[:, gs])
        mu = jnp.mean(v, axis=-1, keepdims=True)
        var = jnp.mean(jnp.square(v - mu), axis=-1, keepdims=True)
        vn = (v - mu) * lax.rsqrt(var + LN_EPS) * nw_ref[:, gs] + nb_ref[:, gs]
        ws = jnp.where(causal, w_ref[g], 0.0)
        mixed = jnp.dot(ws, vn, preferred_element_type=F32) + b_ref[:, g:g + 1]
        o_ref[:, gs] = (u * mixed).astype(o_ref.dtype)


def _spatial_gating(tail, norm_w, norm_b, sg_w, sg_b):
    T = tail.shape[0]
    C = SG_CHUNK
    return pl.pallas_call(
        _sg_kernel,
        grid=(T // C,),
        in_specs=[pl.BlockSpec((C, SG_W), lambda i: (i, 0)),
                  pl.BlockSpec((C, SG_W), lambda i: (i, 1)),
                  pl.BlockSpec((1, SG_W), lambda i: (0, 0)),
                  pl.BlockSpec((1, SG_W), lambda i: (0, 0)),
                  pl.BlockSpec((SG_GROUPS, C, C), lambda i: (0, 0, 0)),
                  pl.BlockSpec((C, SG_GROUPS), lambda i: (0, 0))],
        out_specs=pl.BlockSpec((C, SG_W), lambda i: (i, 0)),
        out_shape=jax.ShapeDtypeStruct((T, SG_W), BF16),
        compiler_params=_cparams(("arbitrary",)),
        name="spatial_gating",
    )(tail, tail, norm_w.reshape(1, SG_W), norm_b.reshape(1, SG_W), sg_w, sg_b.T)


def _layer_norm_rows(r, w, b):
    mu = jnp.mean(r, axis=-1, keepdims=True)
    d = r - mu
    var = jnp.mean(d * d, axis=-1, keepdims=True)
    return d * lax.rsqrt(var + LN_EPS) * w + b


def _outproj_kernel(x_ref, ysb_ref, ygdn_ref, ysg_ref, w_ref, gt_ref, lw_ref, lb_ref,
                    sc_ref, sh_ref, xo_ref, ho_ref, *, alpha):
    y = jnp.dot(ysb_ref[...], w_ref[0:SB_W, :], preferred_element_type=F32)
    y = y + jnp.dot(ygdn_ref[...], w_ref[SB_W:SB_W + GDN_W, :], preferred_element_type=F32)
    y = y + jnp.dot(ysg_ref[...], w_ref[SB_W + GDN_W:, :], preferred_element_type=F32)
    r = alpha * x_ref[...] + (1.0 + gt_ref[...]) * y
    xn = _layer_norm_rows(r, lw_ref[...], lb_ref[...])
    xo_ref[...] = xn
    h2 = xn * (1.0 + sc_ref[...]) + sh_ref[...]
    half = h2.shape[1] // 2
    ho_ref[...] = _pack_bf16_pair(h2[:, :half], h2[:, half:])


def _outproj_ln(x, y_sb, y_gdn, y_sg, w_out_bf16, mod, ln_w, ln_b, alpha):
    T, D = x.shape
    tm = 256
    row = lambda idx: pl.BlockSpec((1, D), lambda i: (0, idx))
    return pl.pallas_call(
        functools.partial(_outproj_kernel, alpha=alpha),
        grid=(T // tm,),
        in_specs=[pl.BlockSpec((tm, D), lambda i: (i, 0)),
                  pl.BlockSpec((tm, SB_W), lambda i: (i, 0)),
                  pl.BlockSpec((tm, GDN_W), lambda i: (i, 0)),
                  pl.BlockSpec((tm, SG_W), lambda i: (i, 0)),
                  pl.BlockSpec((D, D), lambda i: (0, 0)),
                  row(2), row(0), row(0), row(4), row(3)],
        out_specs=[pl.BlockSpec((tm, D), lambda i: (i, 0)),
                   pl.BlockSpec((tm, D // 2), lambda i: (i, 0))],
        out_shape=[jax.ShapeDtypeStruct((T, D), F32), jax.ShapeDtypeStruct((T, D // 2), I32)],
        compiler_params=_cparams(("arbitrary",)),
        name="outproj_ln",
    )(x, y_sb, y_gdn, y_sg, w_out_bf16, mod, ln_w.reshape(1, D), ln_b.reshape(1, D), mod, mod)


def _first_max(vals, idx, axis):
    m = jnp.max(vals, axis=axis, keepdims=True)
    big = jnp.iinfo(jnp.int32).max
    first = jnp.min(jnp.where(vals == m, idx, big), axis=axis, keepdims=True)
    return m, first


def _router_kernel(x_ref, sc_ref, sh_ref, wt_ref, b_ref, idx_ref, wgt_ref, rank_ref, cnt_ref, carry_ref):
    E, G, S = N_EXPERTS, N_GROUPS, GROUP_SIZE
    tm = x_ref.shape[0]
    step = pl.program_id(0)

    @pl.when(step == 0)
    def _():
        carry_ref[...] = jnp.zeros_like(carry_ref)

    h = x_ref[...] * (1.0 + sc_ref[...]) + sh_ref[...]
    h_hi = h.astype(BF16)
    h_lo = (h - h_hi.astype(F32)).astype(BF16)
    wt = wt_ref[...]
    w_hi = wt.astype(BF16)
    w_lo = (wt - w_hi.astype(F32)).astype(BF16)
    nt = (((1,), (1,)), ((), ()))
    logits = (lax.dot_general(w_hi, h_hi, nt, preferred_element_type=F32)
              + lax.dot_general(w_hi, h_lo, nt, preferred_element_type=F32)
              + lax.dot_general(w_lo, h_hi, nt, preferred_element_type=F32))
    scores = _sigmoid(logits)
    sel = scores + b_ref[:, 0:1]

    sel3 = sel.reshape(G, S, tm)
    e_in_g = _iota((G, S, tm), 1)
    m1, i1 = _first_max(sel3, e_in_g, 1)
    m2 = jnp.max(jnp.where(e_in_g == i1, -jnp.inf, sel3), axis=1, keepdims=True)
    gs = m1 + m2

    g_idx = _iota((G, 1, tm), 0)
    gmask = jnp.zeros((G, 1, tm), jnp.bool_)
    for _ in range(TOPK_GROUPS):
        _, gi = _first_max(gs, g_idx, 0)
        hit = g_idx == gi
        gmask = jnp.logical_or(gmask, hit)
        gs = jnp.where(hit, -jnp.inf, gs)
    cur = jnp.where(gmask, sel3, -jnp.inf).reshape(E, tm)

    e_idx = _iota((E, tm), 0)
    chosen = jnp.zeros((E, tm), jnp.bool_)
    idx_rows = []
    score_rows = []
    for _ in range(TOP_K):
        _, ei = _first_max(cur, e_idx, 0)
        hit = e_idx == ei
        chosen = jnp.logical_or(chosen, hit)
        cur = jnp.where(hit, -jnp.inf, cur)
        idx_rows.append(ei)
        score_rows.append(jnp.sum(jnp.where(hit, scores, 0.0), axis=0, keepdims=True))
    top_idx = jnp.concatenate(idx_rows, axis=0)
    top_sc = jnp.concatenate(score_rows, axis=0)
    idx_ref[...] = top_idx
    wgt_ref[...] = top_sc / jnp.sum(top_sc, axis=0, keepdims=True) * ROUTED_SCALE

    chosen_b = chosen.astype(BF16)
    before = (_iota((tm, tm), 0) < _iota((tm, tm), 1)).astype(BF16)
    excl = jnp.dot(chosen_b, before, preferred_element_type=F32) + carry_ref[:, 0:1]
    rank_rows = [jnp.sum(jnp.where(e_idx == idx_rows[k], excl, 0.0), axis=0, keepdims=True)
                 for k in range(TOP_K)]
    rank_ref[...] = jnp.concatenate(rank_rows, axis=0).astype(I32)
    carry_ref[...] = carry_ref[...] + jnp.sum(chosen.astype(F32), axis=1, keepdims=True)
    cnt_ref[...] = carry_ref[...]


def _router(x, mod, w_router_t, router_bias):
    T, D = x.shape
    tm = ROUTER_TM
    E, K = N_EXPERTS, TOP_K
    bias = jnp.broadcast_to(router_bias.reshape(E, 1), (E, LANES))
    tok = lambda: pl.BlockSpec((K, tm), lambda i: (0, i))
    return pl.pallas_call(
        _router_kernel,
        grid=(T // tm,),
        in_specs=[pl.BlockSpec((tm, D), lambda i: (i, 0)),
                  pl.BlockSpec((1, D), lambda i: (0, 4)),
                  pl.BlockSpec((1, D), lambda i: (0, 3)),
                  pl.BlockSpec((E, D), lambda i: (0, 0)),
                  pl.BlockSpec((E, LANES), lambda i: (0, 0))],
        out_specs=[tok(), tok(), tok(), pl.BlockSpec((E, LANES), lambda i: (0, 0))],
        out_shape=[jax.ShapeDtypeStruct((K, T), I32), jax.ShapeDtypeStruct((K, T), F32),
                   jax.ShapeDtypeStruct((K, T), I32), jax.ShapeDtypeStruct((E, LANES), F32)],
        scratch_shapes=[pltpu.VMEM((E, LANES), F32)],
        compiler_params=_cparams(("arbitrary",)),
        name="router",
    )(x, mod, mod, w_router_t, bias)


def _plan_kernel(cnt_ref, idx_ref, rank_ref, pos_ref, texp_ref, end_ref, *, tile, n_tiles):
    E = N_EXPERTS
    tm = idx_ref.shape[1]
    cnt = cnt_ref[...]
    ntile = jnp.floor((cnt + (tile - 1)) * (1.0 / tile))
    lower = (_iota((E, E), 0) >= _iota((E, E), 1)).astype(F32)
    tile_end = jnp.dot(lower, ntile, preferred_element_type=F32)
    row_start = (tile_end - ntile) * tile
    e_idx = _iota((E, tm), 0)
    start_col = row_start[:, 0:1]
    for k in range(TOP_K):
        base = jnp.sum(jnp.where(e_idx == idx_ref[k:k + 1, :], start_col, 0.0), axis=0, keepdims=True)
        pos_ref[k:k + 1, :] = base.astype(I32) + rank_ref[k:k + 1, :]
    tile_i = _iota((E, n_tiles), 1).astype(F32)
    texp = jnp.sum((tile_end[:, 0:1] <= tile_i).astype(F32), axis=0, keepdims=True)
    texp_ref[...] = jnp.minimum(texp, E - 1.0).astype(I32)
    end_ref[...] = (tile_end * tile).astype(I32)


def _plan(cnt, idx_t, rank_t, tile, n_tiles):
    K, T = idx_t.shape
    E = N_EXPERTS
    tm = min(2048, T)
    tok = lambda: pl.BlockSpec((K, tm), lambda i: (0, i))
    return pl.pallas_call(
        functools.partial(_plan_kernel, tile=tile, n_tiles=n_tiles),
        grid=(T // tm,),
        in_specs=[pl.BlockSpec((E, LANES), lambda i: (0, 0)), tok(), tok()],
        out_specs=[tok(), pl.BlockSpec((1, n_tiles), lambda i: (0, 0)),
                   pl.BlockSpec((E, LANES), lambda i: (0, 0))],
        out_shape=[jax.ShapeDtypeStruct((K, T), I32), jax.ShapeDtypeStruct((1, n_tiles), I32),
                   jax.ShapeDtypeStruct((E, LANES), I32)],
        compiler_params=_cparams(("arbitrary",)),
        name="route_plan",
    )(cnt, idx_t, rank_t)


def _invert_kernel(pos_ref, init_ref, inv_ref, sem, *, n_pairs):
    fill = pltpu.make_async_copy(init_ref, inv_ref, sem)
    fill.start()
    fill.wait()

    def scatter(q, c):
        inv_ref[pos_ref[q]] = q
        return c

    lax.fori_loop(0, n_pairs, scatter, 0, unroll=8)


def _invert(pos_flat, inv_init):
    n_pairs = pos_flat.shape[0]
    return pl.pallas_call(
        functools.partial(_invert_kernel, n_pairs=n_pairs),
        in_specs=[pl.BlockSpec(memory_space=pltpu.SMEM), pl.BlockSpec(memory_space=pl.ANY)],
        out_specs=pl.BlockSpec(memory_space=pltpu.SMEM),
        out_shape=jax.ShapeDtypeStruct(inv_init.shape, I32),
        scratch_shapes=[pltpu.SemaphoreType.DMA(())],
        name="route_invert",
    )(pos_flat, inv_init)


def _expert_kernel(texp_ref, used_ref, inv_ref, h_ref, w1_ref, w3_ref, w2_ref, ys_ref,
                   xbuf, obuf, gsem, ssem, *, tm, n_tok, n_tiles):
    i = pl.program_id(0)
    used = used_ref[0]
    slot = lax.rem(i, 2)
    other = 1 - slot
    W = xbuf.shape[2]
    F = w1_ref.shape[2]

    def gather_start(tile_idx, s, rows):
        for j in rows:
            tok = inv_ref[tile_idx * tm + j] & (n_tok - 1)
            pltpu.make_async_copy(h_ref.at[pl.ds(tok, 1)], xbuf.at[s, pl.ds(j, 1)], gsem.at[s]).start()

    def gather_wait(s):
        pltpu.make_async_copy(h_ref.at[pl.ds(0, tm)], xbuf.at[s], gsem.at[s]).wait()

    def scatter_start(tile_idx, s, rows):
        for j in rows:
            dst = inv_ref[tile_idx * tm + j]
            pltpu.make_async_copy(obuf.at[s, pl.ds(j, 1)], ys_ref.at[pl.ds(dst, 1)], ssem.at[s]).start()

    def scatter_wait(s):
        pltpu.make_async_copy(obuf.at[s], ys_ref.at[pl.ds(0, tm)], ssem.at[s]).wait()

    @pl.when(i == 0)
    def _():
        obuf[1] = jnp.zeros((tm, W), obuf.dtype)
        gather_start(0, 0, range(tm))

    @pl.when(jnp.logical_and(i >= 1, i < used))
    def _():
        scatter_wait(slot)

    @pl.when(i < used)
    def _():
        gather_wait(slot)
        nxt = jnp.minimum(i + 1, used - 1)
        prv = jnp.where(i == 0, n_tiles, i - 1)
        chunks = EXPERT_K_CHUNKS // 2
        per = tm // (4 * chunks)
        kc = W // chunks

        def dma_group(gi):
            rows = range(gi * per, (gi + 1) * per)
            gather_start(nxt, other, rows)
            scatter_start(prv, other, rows)

        h1 = jnp.zeros((tm, F), F32)
        h3 = jnp.zeros((tm, F), F32)
        for c in range(chunks):
            parts = _unpack_bf16_pair(xbuf[slot, :, c * kc:(c + 1) * kc])
            for p, (part, k0) in enumerate(zip(parts, (c * kc, W + c * kc))):
                dma_group(2 * c + p)
                xc = part.astype(BF16)
                h1 = h1 + jnp.dot(xc, w1_ref[0, k0:k0 + kc, :].astype(BF16), preferred_element_type=F32)
                h3 = h3 + jnp.dot(xc, w3_ref[0, k0:k0 + kc, :].astype(BF16), preferred_element_type=F32)
        a = (_silu(h1) * h3).astype(BF16)
        for c in range(chunks):
            halves = []
            for p, n0 in enumerate((c * kc, W + c * kc)):
                dma_group(2 * chunks + 2 * c + p)
                halves.append(jnp.dot(a, w2_ref[0, :, n0:n0 + kc].astype(BF16), preferred_element_type=F32))
            obuf[slot, :, c * kc:(c + 1) * kc] = _pack_bf16_pair(*halves)

    @pl.when(i == used - 1)
    def _():
        scatter_wait(other)
        scatter_start(i, slot, range(tm))
        gather_wait(other)
        scatter_wait(slot)


def _experts(tile_expert, tiles_used, inv, h2, w1, w3, w2, layer):
    T, W = h2.shape
    D, F = w1.shape[-2:]
    assert 2 * W == D
    tm = EXPERT_TM
    assert T & (T - 1) == 0
    n_tiles = inv.shape[0] // tm - 1
    wspec = lambda a, b: pl.BlockSpec((None, 1, a, b), lambda i, te, nu, iv: (layer, te[i], 0, 0))
    return pl.pallas_call(
        functools.partial(_expert_kernel, tm=tm, n_tok=T, n_tiles=n_tiles),
        grid_spec=pltpu.PrefetchScalarGridSpec(
            num_scalar_prefetch=3,
            grid=(n_tiles,),
            in_specs=[pl.BlockSpec(memory_space=pl.ANY), wspec(D, F), wspec(D, F), wspec(F, D)],
            out_specs=pl.BlockSpec(memory_space=pl.ANY),
            scratch_shapes=[pltpu.VMEM((2, tm, W), I32), pltpu.VMEM((2, tm, W), I32),
                            pltpu.SemaphoreType.DMA((2,)), pltpu.SemaphoreType.DMA((2,))]),
        out_shape=jax.ShapeDtypeStruct((T * TOP_K + tm, W), I32),
        compiler_params=_cparams(("arbitrary",)),
        name="moe_experts",
    )(tile_expert, tiles_used, inv, h2, w1, w3, w2)


def _combine_kernel(*refs, alpha):
    y_refs = refs[:TOP_K]
    (x_ref, h_ref, g_ref, s1_ref, s3_ref, s2_ref, gt_ref, lw_ref, lb_ref, sc_ref, sh_ref,
     xo_ref, ho_ref) = refs[TOP_K:]
    W = h_ref.shape[1]
    h_lo, h_hi = (p.astype(BF16) for p in _unpack_bf16_pair(h_ref[...]))
    up = lambda w_ref: (jnp.dot(h_lo, w_ref[:W, :], preferred_element_type=F32)
                        + jnp.dot(h_hi, w_ref[W:, :], preferred_element_type=F32))
    a = _silu(up(s1_ref)) * up(s3_ref)
    shared = jnp.dot(a.astype(BF16), s2_ref[...], preferred_element_type=F32)
    g = g_ref[...]
    y_lo, y_hi = shared[:, :W], shared[:, W:]
    for k in range(TOP_K):
        e_lo, e_hi = _unpack_bf16_pair(y_refs[k][...])
        y_lo = y_lo + g[:, k:k + 1] * e_lo
        y_hi = y_hi + g[:, k:k + 1] * e_hi
    y = jnp.concatenate([y_lo, y_hi], axis=1)
    r = alpha * x_ref[...] + (1.0 + gt_ref[...]) * y
    xn = _layer_norm_rows(r, lw_ref[...], lb_ref[...])
    xo_ref[...] = xn
    ho_ref[...] = (xn * (1.0 + sc_ref[...]) + sh_ref[...]).astype(ho_ref.dtype)


def _combine(ys, x, h2, gates, s1, s3, s2, mod, ln_w, ln_b, mod_next, alpha):
    T, D = x.shape
    F = s1.shape[-1]
    tc = COMBINE_TC
    nb = T // tc
    row = lambda idx: pl.BlockSpec((1, D), lambda i: (0, idx))
    full = lambda a, b: pl.BlockSpec((a, b), lambda i: (0, 0))
    tok = lambda: pl.BlockSpec((tc, D), lambda i: (i, 0))
    pair = lambda k: pl.BlockSpec((tc, D // 2), lambda i: (k * nb + i, 0))
    return pl.pallas_call(
        functools.partial(_combine_kernel, alpha=alpha),
        grid=(nb,),
        in_specs=[pair(k) for k in range(TOP_K)] + [
            tok(), pair(0), pl.BlockSpec((tc, TOP_K), lambda i: (i, 0)),
            full(D, F), full(D, F), full(F, D), row(5), row(0), row(0), row(1), row(0)],
        out_specs=[tok(), tok()],
        out_shape=[jax.ShapeDtypeStruct((T, D), F32), jax.ShapeDtypeStruct((T, D), BF16)],
        compiler_params=_cparams(("arbitrary",)),
        name="moe_combine",
    )(*([ys] * TOP_K), x, h2, gates, s1, s3, s2, mod, ln_w.reshape(1, D), ln_b.reshape(1, D),
      mod_next, mod_next)


def kernel(x, c, w_ada, b_ada, w_in, conv_w, gdn_a_log, gdn_dt_bias, gdn_norm_w, sg_norm_w, sg_norm_b, sg_w, sg_b, w_out, ln1_w, ln1_b, w_router, router_bias, exp_w1, exp_w3, exp_w2, sh_w1, sh_w3, sh_w2, ln2_w, ln2_b):
    B, T, D = x.shape
    assert B == 1
    L = w_ada.shape[0]
    alpha = (2.0 * L) ** 0.25
    main_cols = 3 * SB_W + 4 * GDN_W
    assert main_cols % 512 == 0 and w_in.shape[2] == main_cols + 2 * GDN_HEADS + 2 * SG_W
    n_tiles = (T * TOP_K) // EXPERT_TM + N_EXPERTS
    n_rows = n_tiles * EXPERT_TM
    n_tiles_pad = -(-n_tiles // LANES) * LANES

    xt = x.reshape(T, D)
    mod_all = _ada_mod(c, w_ada, b_ada)

    ab_cols = w_in[:, :, main_cols:main_cols + 2 * GDN_HEADS]
    ab_cols = jnp.pad(ab_cols, ((0, 0), (0, 0), (0, LANES - 2 * GDN_HEADS)))
    w_tail = jnp.concatenate([w_in[:, :, main_cols + 2 * GDN_HEADS:], ab_cols], axis=-1)
    w_out_b = w_out.astype(BF16)
    w_router_t = jnp.swapaxes(w_router, 1, 2)
    s1, s3, s2 = sh_w1.astype(BF16), sh_w3.astype(BF16), sh_w2.astype(BF16)

    inv_init = T * TOP_K + lax.rem(jnp.arange(n_rows + EXPERT_TM, dtype=I32), EXPERT_TM)

    h = _modulate(xt, mod_all[0], 1, 0)
    for l in range(L):
        mod = mod_all[l]
        mod_next = mod_all[min(l + 1, L - 1)]
        sb_qkv = _proj(h, w_in, l, 0, 3 * SB_W // 512, 512, BF16)
        proj_g = _proj(h, w_in, l, 3 * SB_W // 512, 4 * GDN_W // 512, 512, F32)
        tail = _proj(h, w_tail, l, 0, 1, w_tail.shape[2], F32)
        y_sb = _sb_attention(sb_qkv)
        y_gdn = _gdn(proj_g, tail, conv_w[l], gdn_a_log[l], gdn_dt_bias[l], gdn_norm_w[l])
        y_sg = _spatial_gating(tail, sg_norm_w[l], sg_norm_b[l], sg_w[l], sg_b[l])
        xt, h2 = _outproj_ln(xt, y_sb, y_gdn, y_sg, w_out_b[l], mod, ln1_w[l], ln1_b[l], alpha)
        idx_t, wgt_t, rank_t, cnt = _router(xt, mod, w_router_t[l], router_bias[l])
        pos_t, tile_expert, seg_end = _plan(cnt, idx_t, rank_t, EXPERT_TM, n_tiles_pad)
        inv = _invert(pos_t.reshape(-1), inv_init)
        tiles_used = seg_end[N_EXPERTS - 1:, 0] // EXPERT_TM
        ys = _experts(tile_expert.reshape(-1), tiles_used, inv, h2, exp_w1, exp_w3, exp_w2, l)
        xt, h = _combine(ys, xt, h2, wgt_t.T, s1[l], s3[l], s2[l], mod, ln2_w[l], ln2_b[l],
                         mod_next, alpha)
    return xt.reshape(B, T, D)
```

```python
import functools

import jax
import jax.numpy as jnp
from jax import lax
from jax.experimental import pallas as pl
from jax.experimental.pallas import tpu as pltpu

F32 = jnp.float32
BF16 = jnp.bfloat16
I32 = jnp.int32
HI16 = -65536

LANES = 128
SUBLANES = 8
VMEM_LIMIT = 56 * 1024 * 1024

HEAD_DIM = 128
SB_HEADS = 4
GDN_HEADS = 8
SG_GROUPS = 4
SB_W = SB_HEADS * HEAD_DIM
GDN_W = GDN_HEADS * HEAD_DIM
SG_W = SG_GROUPS * HEAD_DIM
GDN_CHUNK = 64
CONV_W = 4
SG_CHUNK = 128
N_EXPERTS = 64
TOP_K = 8
N_GROUPS = 8
GROUP_SIZE = N_EXPERTS // N_GROUPS
TOPK_GROUPS = 4
ROUTED_SCALE = 2.5
LN_EPS = 1e-5

SB_TQ = 256
SB_TK = 256
SB_UNDERFLOW = 110.0
EXPERT_TM = 256
ROUTER_TM = 512
EXPERT_K_CHUNKS = 8
ROW_TILE = SUBLANES
COMBINE_TC = 128


def _cparams(sem):
    return pltpu.CompilerParams(dimension_semantics=sem, vmem_limit_bytes=VMEM_LIMIT)


def _sigmoid(x):
    return 1.0 / (1.0 + jnp.exp(-x))


def _silu(x):
    return x * _sigmoid(x)


def _softplus(x):
    return jnp.maximum(x, 0.0) + jnp.log(1.0 + jnp.exp(-jnp.abs(x)))


def _iota(shape, dim):
    return lax.broadcasted_iota(I32, shape, dim)


def _pack_bf16_pair(lo, hi):
    lo_bits = lax.bitcast_convert_type(lo.astype(BF16).astype(F32), I32)
    hi_bits = lax.bitcast_convert_type(hi.astype(BF16).astype(F32), I32)
    return jnp.bitwise_or(jnp.bitwise_and(hi_bits, HI16), lax.shift_right_logical(lo_bits, 16))


def _unpack_bf16_pair(words):
    lo = lax.bitcast_convert_type(jnp.left_shift(words, 16), F32)
    hi = lax.bitcast_convert_type(jnp.bitwise_and(words, HI16), F32)
    return lo, hi


def _ada_kernel(c_ref, w_ref, b_ref, o_ref):
    cb = _silu(c_ref[...])
    tn = w_ref.shape[2]
    for j in range(tn // LANES):
        sl = slice(j * LANES, (j + 1) * LANES)
        o_ref[0, :, sl] = jnp.sum(w_ref[0, :, sl] * cb, axis=0, keepdims=True) + b_ref[0, :, sl]


def _ada_mod(c, w_ada, b_ada):
    L, D, N = w_ada.shape
    tn = 1536
    cb = jnp.broadcast_to(c.reshape(D, 1), (D, LANES))
    return pl.pallas_call(
        _ada_kernel,
        grid=(L, N // tn),
        in_specs=[pl.BlockSpec((D, LANES), lambda l, j: (0, 0)),
                  pl.BlockSpec((1, D, tn), lambda l, j: (l, 0, j)),
                  pl.BlockSpec((1, 1, tn), lambda l, j: (l, 0, j))],
        out_specs=pl.BlockSpec((1, 1, tn), lambda l, j: (l, 0, j)),
        out_shape=jax.ShapeDtypeStruct((L, 1, N), F32),
        compiler_params=_cparams(("arbitrary", "arbitrary")),
        name="ada_mod",
    )(cb, w_ada, b_ada.reshape(L, 1, N))


def _modulate_kernel(x_ref, sc_ref, sh_ref, o_ref):
    o_ref[...] = (x_ref[...] * (1.0 + sc_ref[...]) + sh_ref[...]).astype(o_ref.dtype)


def _modulate(x, mod, sc_idx, sh_idx):
    T, D = x.shape
    tm = 512
    return pl.pallas_call(
        _modulate_kernel,
        grid=(T // tm,),
        in_specs=[pl.BlockSpec((tm, D), lambda i: (i, 0)),
                  pl.BlockSpec((1, D), lambda i: (0, sc_idx)),
                  pl.BlockSpec((1, D), lambda i: (0, sh_idx))],
        out_specs=pl.BlockSpec((tm, D), lambda i: (i, 0)),
        out_shape=jax.ShapeDtypeStruct((T, D), BF16),
        compiler_params=_cparams(("arbitrary",)),
        name="modulate",
    )(x, mod, mod)


def _proj_kernel(x_ref, w_ref, o_ref, wb_ref):
    @pl.when(pl.program_id(1) == 0)
    def _():
        wb_ref[...] = w_ref[0].astype(BF16)

    o_ref[...] = jnp.dot(x_ref[...], wb_ref[...], preferred_element_type=F32).astype(o_ref.dtype)


def _proj(h, w, layer, col_block0, n_blocks, tn, out_dtype):
    T, K = h.shape
    tm = min(1024, T)
    return pl.pallas_call(
        _proj_kernel,
        grid=(n_blocks, T // tm),
        in_specs=[pl.BlockSpec((tm, K), lambda j, i: (i, 0)),
                  pl.BlockSpec((1, K, tn), lambda j, i: (layer, 0, j + col_block0))],
        out_specs=pl.BlockSpec((tm, tn), lambda j, i: (i, j)),
        out_shape=jax.ShapeDtypeStruct((T, n_blocks * tn), out_dtype),
        scratch_shapes=[pltpu.VMEM((K, tn), BF16)],
        compiler_params=_cparams(("arbitrary", "arbitrary")),
        name="in_proj",
    )(h, w)


def _sb_kernel(q_ref, k_ref, v_ref, o_ref, *, tq, tk, scale):
    i = pl.program_id(1)
    q = q_ref[...]
    tri = (_iota((tk, tk), 0) >= _iota((tk, tk), 1)).astype(BF16)

    def block(j, r, diagonal):
        ks = pl.multiple_of(j * tk, tk)
        k = k_ref[pl.ds(ks, tk), :]
        v = v_ref[pl.ds(ks, tk), :]
        z = lax.dot_general(q, k, (((1,), (1,)), ((), ())), preferred_element_type=F32) * scale
        sp = _softplus(z)
        if diagonal:
            causal = _iota((tq, tk), 1) < _iota((tq, tk), 0)
            sp = jnp.where(causal, sp, 0.0)
        sp_hi = sp.astype(BF16)
        sp_lo = (sp - sp_hi.astype(F32)).astype(BF16)
        c = (jnp.dot(sp_hi, tri, preferred_element_type=F32)
             + jnp.dot(sp_lo, tri, preferred_element_type=F32))
        a = jnp.exp(z - c - r)
        if diagonal:
            a = jnp.where(causal, a, 0.0)
        pv = jnp.dot(a.astype(BF16), v, preferred_element_type=F32)
        return pv, r + c[:, 0:1]

    acc, r = block(i, jnp.zeros((tq, 1), F32), True)

    def cond(carry):
        step, _, _, rmin = carry
        return jnp.logical_and(step <= i, rmin < SB_UNDERFLOW)

    def body(carry):
        step, acc, r, _ = carry
        pv, r = block(i - step, r, False)
        return step + 1, acc + pv, r, jnp.min(r)

    _, acc, _, _ = lax.while_loop(cond, body, (jnp.int32(1), acc, r, jnp.min(r)))
    o_ref[...] = acc.astype(o_ref.dtype)


def _sb_attention(qkv):
    T = qkv.shape[0]
    tq, tk = SB_TQ, SB_TK
    assert tq == tk
    kern = functools.partial(_sb_kernel, tq=tq, tk=tk, scale=HEAD_DIM ** -0.5)
    return pl.pallas_call(
        kern,
        grid=(SB_HEADS, T // tq),
        in_specs=[pl.BlockSpec((tq, HEAD_DIM), lambda h, i: (i, h)),
                  pl.BlockSpec((T, HEAD_DIM), lambda h, i: (0, SB_HEADS + h)),
                  pl.BlockSpec((T, HEAD_DIM), lambda h, i: (0, 2 * SB_HEADS + h))],
        out_specs=pl.BlockSpec((tq, HEAD_DIM), lambda h, i: (i, h)),
        out_shape=jax.ShapeDtypeStruct((T, SB_W), BF16),
        compiler_params=_cparams(("arbitrary", "arbitrary")),
        name="sb_attention",
    )(qkv, qkv, qkv)


def _gdn_kernel(qkv_ref, z_ref, ab_ref, cw_ref, alog_ref, dtb_ref, nw_ref, o_ref,
                tail_ref, state_ref):
    C = GDN_CHUNK
    Dh = HEAD_DIM
    step = pl.program_id(0)

    @pl.when(step == 0)
    def _():
        tail_ref[...] = jnp.zeros_like(tail_ref)
        state_ref[...] = jnp.zeros_like(state_ref)

    x = qkv_ref[...]
    ext = jnp.concatenate([tail_ref[...], x], axis=0)
    cw = cw_ref[...]
    conv = cw[CONV_W - 1:CONV_W, :] * x
    for s in range(1, CONV_W):
        conv = conv + cw[CONV_W - 1 - s:CONV_W - s, :] * ext[SUBLANES - s:SUBLANES - s + C, :]
    tail_ref[...] = x[C - SUBLANES:, :]
    act = _silu(conv)

    ab = ab_ref[...]
    g = -jnp.exp(alog_ref[...]) * _softplus(ab + dtb_ref[...])
    beta_all = _sigmoid(ab)
    ri = _iota((C, C), 0)
    ci = _iota((C, C), 1)
    tril = (ri >= ci)
    strict = (ri > ci)
    gc = jnp.dot(tril.astype(F32), g, preferred_element_type=F32,
                 precision=lax.Precision.HIGHEST)
    gc_t = jnp.transpose(gc)
    eye = (ri == ci).astype(F32)
    nw = nw_ref[...]
    zz = z_ref[...]

    H = range(GDN_HEADS)
    nt = (((1,), (1,)), ((), ()))
    mm = lambda a, b: jnp.dot(a, b, preferred_element_type=F32)
    qs = [act[:, h * Dh:(h + 1) * Dh] for h in H]
    ks = [act[:, GDN_W + h * Dh:GDN_W + (h + 1) * Dh] for h in H]
    vs = [act[:, 2 * GDN_W + h * Dh:2 * GDN_W + (h + 1) * Dh] for h in H]
    qs = [q * lax.rsqrt(jnp.sum(q * q, axis=-1, keepdims=True) + 1e-6) * (Dh ** -0.5) for q in qs]
    ks = [k * lax.rsqrt(jnp.sum(k * k, axis=-1, keepdims=True) + 1e-6) for k in ks]
    beta = [beta_all[:, SUBLANES + h:SUBLANES + h + 1] for h in H]
    gcol = [gc[:, h:h + 1] for h in H]
    grow = [gc_t[h:h + 1, :] for h in H]
    glast = [gc[C - 1:C, h:h + 1] for h in H]
    decay = [jnp.exp(jnp.where(tril, gcol[h] - grow[h], -jnp.inf)) for h in H]
    eg = [jnp.exp(gcol[h]) for h in H]
    kb = [ks[h] * beta[h] for h in H]
    kk = [lax.dot_general(kb[h], ks[h], nt, preferred_element_type=F32) for h in H]
    xm = [-jnp.where(strict, kk[h] * decay[h], 0.0) for h in H]
    tm = [eye + xm[h] for h in H]
    p = xm
    for _ in range(5):
        p = [mm(p[h], p[h]) for h in H]
        tm = [tm[h] + mm(tm[h], p[h]) for h in H]
    sol = [mm(tm[h], jnp.concatenate([vs[h] * beta[h], kb[h] * eg[h]], axis=1)) for h in H]
    qk = [lax.dot_general(qs[h], ks[h], nt, preferred_element_type=F32) for h in H]
    qk = [jnp.where(tril, qk[h] * decay[h], 0.0) for h in H]
    k_dec = [ks[h] * jnp.exp(glast[h] - gcol[h]) for h in H]
    state = [state_ref[h] for h in H]
    ws = [mm(jnp.concatenate([sol[h][:, Dh:], qs[h] * eg[h]], axis=0), state[h]) for h in H]
    v_new = [sol[h][:, :Dh] - ws[h][:C] for h in H]
    o = [ws[h][C:] + mm(qk[h], v_new[h]) for h in H]
    for h in H:
        state_ref[h] = state[h] * jnp.exp(glast[h]) + lax.dot_general(
            k_dec[h], v_new[h], (((0,), (0,)), ((), ())), preferred_element_type=F32)
    for h in H:
        hs = slice(h * Dh, (h + 1) * Dh)
        on = o[h] * lax.rsqrt(jnp.mean(o[h] * o[h], axis=-1, keepdims=True) + 1e-6)
        o_ref[:, hs] = (on * nw * _silu(zz[:, hs])).astype(o_ref.dtype)


def _gdn(proj_g, tail, conv_w, a_log, dt_bias, norm_w):
    T = proj_g.shape[0]
    C = GDN_CHUNK
    pad = lambda t: jnp.zeros((1, LANES), F32).at[0, :GDN_HEADS].set(t)
    ab_block = tail.shape[1] // LANES - 1
    return pl.pallas_call(
        _gdn_kernel,
        grid=(T // C,),
        in_specs=[pl.BlockSpec((C, 3 * GDN_W), lambda i: (i, 0)),
                  pl.BlockSpec((C, GDN_W), lambda i: (i, 3)),
                  pl.BlockSpec((C, LANES), lambda i: (i, ab_block)),
                  pl.BlockSpec((CONV_W, 3 * GDN_W), lambda i: (0, 0)),
                  pl.BlockSpec((1, LANES), lambda i: (0, 0)),
                  pl.BlockSpec((1, LANES), lambda i: (0, 0)),
                  pl.BlockSpec((1, HEAD_DIM), lambda i: (0, 0))],
        out_specs=pl.BlockSpec((C, GDN_W), lambda i: (i, 0)),
        out_shape=jax.ShapeDtypeStruct((T, GDN_W), BF16),
        scratch_shapes=[pltpu.VMEM((SUBLANES, 3 * GDN_W), F32),
                        pltpu.VMEM((GDN_HEADS, HEAD_DIM, HEAD_DIM), F32)],
        compiler_params=_cparams(("arbitrary",)),
        name="gdn",
    )(proj_g, proj_g, tail, conv_w, pad(a_log), pad(dt_bias), norm_w.reshape(1, HEAD_DIM))


def _gelu(x):
    return 0.5 * x * (1.0 + lax.erf(x * (2.0 ** -0.5)))


def _sg_kernel(u_ref, v_ref, nw_ref, nb_ref, w_ref, b_ref, o_ref):
    C = SG_CHUNK
    causal = _iota((C, C), 0) >= _iota((C, C), 1)
    for g in range(SG_GROUPS):
        gs = slice(g * HEAD_DIM, (g + 1) * HEAD_DIM)
        u = _gelu(u_ref[:, gs])
        v = _gelu(v_ref[:, gs])
        mu = jnp.mean(v, axis=-1, keepdims=True)
        var = jnp.mean(jnp.square(v - mu), axis=-1, keepdims=True)
        vn = (v - mu) * lax.rsqrt(var + LN_EPS) * nw_ref[:, gs] + nb_ref[:, gs]
        ws = jnp.where(causal, w_ref[g], 0.0)
        mixed = jnp.dot(ws, vn, preferred_element_type=F32) + b_ref[:, g:g + 1]
        o_ref[:, gs] = (u * mixed).astype(o_ref.dtype)


def _spatial_gating(tail, norm_w, norm_b, sg_w, sg_b):
    T = tail.shape[0]
    C = SG_CHUNK
    return pl.pallas_call(
        _sg_kernel,
        grid=(T // C,),
        in_specs=[pl.BlockSpec((C, SG_W), lambda i: (i, 0)),
                  pl.BlockSpec((C, SG_W), lambda i: (i, 1)),
                  pl.BlockSpec((1, SG_W), lambda i: (0, 0)),
                  pl.BlockSpec((1, SG_W), lambda i: (0, 0)),
                  pl.BlockSpec((SG_GROUPS, C, C), lambda i: (0, 0, 0)),
                  pl.BlockSpec((C, SG_GROUPS), lambda i: (0, 0))],
        out_specs=pl.BlockSpec((C, SG_W), lambda i: (i, 0)),
        out_shape=jax.ShapeDtypeStruct((T, SG_W), BF16),
        compiler_params=_cparams(("arbitrary",)),
        name="spatial_gating",
    )(tail, tail, norm_w.reshape(1, SG_W), norm_b.reshape(1, SG_W), sg_w, sg_b.T)


def _layer_norm_rows(r, w, b):
    mu = jnp.mean(r, axis=-1, keepdims=True)
    d = r - mu
    var = jnp.mean(d * d, axis=-1, keepdims=True)
    return d * lax.rsqrt(var + LN_EPS) * w + b


def _outproj_kernel(x_ref, ysb_ref, ygdn_ref, ysg_ref, w_ref, gt_ref, lw_ref, lb_ref,
                    sc_ref, sh_ref, xo_ref, ho_ref, *, alpha):
    y = jnp.dot(ysb_ref[...], w_ref[0:SB_W, :], preferred_element_type=F32)
    y = y + jnp.dot(ygdn_ref[...], w_ref[SB_W:SB_W + GDN_W, :], preferred_element_type=F32)
    y = y + jnp.dot(ysg_ref[...], w_ref[SB_W + GDN_W:, :], preferred_element_type=F32)
    r = alpha * x_ref[...] + (1.0 + gt_ref[...]) * y
    xn = _layer_norm_rows(r, lw_ref[...], lb_ref[...])
    xo_ref[...] = xn
    h2 = xn * (1.0 + sc_ref[...]) + sh_ref[...]
    half = h2.shape[1] // 2
    words = _pack_bf16_pair(h2[:, :half], h2[:, half:])
    for c in range(ROW_TILE):
        ho_ref[pl.ds(c, h2.shape[0], stride=ROW_TILE), :] = words[:, c * LANES:(c + 1) * LANES]


def _outproj_ln(x, y_sb, y_gdn, y_sg, w_out_bf16, mod, ln_w, ln_b, alpha):
    T, D = x.shape
    tm = 256
    row = lambda idx: pl.BlockSpec((1, D), lambda i: (0, idx))
    return pl.pallas_call(
        functools.partial(_outproj_kernel, alpha=alpha),
        grid=(T // tm,),
        in_specs=[pl.BlockSpec((tm, D), lambda i: (i, 0)),
                  pl.BlockSpec((tm, SB_W), lambda i: (i, 0)),
                  pl.BlockSpec((tm, GDN_W), lambda i: (i, 0)),
                  pl.BlockSpec((tm, SG_W), lambda i: (i, 0)),
                  pl.BlockSpec((D, D), lambda i: (0, 0)),
                  row(2), row(0), row(0), row(4), row(3)],
        out_specs=[pl.BlockSpec((tm, D), lambda i: (i, 0)),
                   pl.BlockSpec((tm * ROW_TILE, LANES), lambda i: (i, 0))],
        out_shape=[jax.ShapeDtypeStruct((T, D), F32), jax.ShapeDtypeStruct((T * ROW_TILE, LANES), I32)],
        compiler_params=_cparams(("arbitrary",)),
        name="outproj_ln",
    )(x, y_sb, y_gdn, y_sg, w_out_bf16, mod, ln_w.reshape(1, D), ln_b.reshape(1, D), mod, mod)


def _first_max(vals, idx, axis):
    m = jnp.max(vals, axis=axis, keepdims=True)
    big = jnp.iinfo(jnp.int32).max
    first = jnp.min(jnp.where(vals == m, idx, big), axis=axis, keepdims=True)
    return m, first


def _router_kernel(x_ref, sc_ref, sh_ref, wt_ref, b_ref, idx_ref, wgt_ref, rank_ref, cnt_ref, carry_ref):
    E, G, S = N_EXPERTS, N_GROUPS, GROUP_SIZE
    tm = x_ref.shape[0]
    step = pl.program_id(0)

    @pl.when(step == 0)
    def _():
        carry_ref[...] = jnp.zeros_like(carry_ref)

    h = x_ref[...] * (1.0 + sc_ref[...]) + sh_ref[...]
    h_hi = h.astype(BF16)
    h_lo = (h - h_hi.astype(F32)).astype(BF16)
    wt = wt_ref[...]
    w_hi = wt.astype(BF16)
    w_lo = (wt - w_hi.astype(F32)).astype(BF16)
    nt = (((1,), (1,)), ((), ()))
    logits = (lax.dot_general(w_hi, h_hi, nt, preferred_element_type=F32)
              + lax.dot_general(w_hi, h_lo, nt, preferred_element_type=F32)
              + lax.dot_general(w_lo, h_hi, nt, preferred_element_type=F32))
    scores = _sigmoid(logits)
    sel = scores + b_ref[:, 0:1]

    sel3 = sel.reshape(G, S, tm)
    e_in_g = _iota((G, S, tm), 1)
    m1, i1 = _first_max(sel3, e_in_g, 1)
    m2 = jnp.max(jnp.where(e_in_g == i1, -jnp.inf, sel3), axis=1, keepdims=True)
    gs = m1 + m2

    g_idx = _iota((G, 1, tm), 0)
    gmask = jnp.zeros((G, 1, tm), jnp.bool_)
    for _ in range(TOPK_GROUPS):
        _, gi = _first_max(gs, g_idx, 0)
        hit = g_idx == gi
        gmask = jnp.logical_or(gmask, hit)
        gs = jnp.where(hit, -jnp.inf, gs)
    cur = jnp.where(gmask, sel3, -jnp.inf).reshape(E, tm)

    e_idx = _iota((E, tm), 0)
    chosen = jnp.zeros((E, tm), jnp.bool_)
    idx_rows = []
    score_rows = []
    for _ in range(TOP_K):
        _, ei = _first_max(cur, e_idx, 0)
        hit = e_idx == ei
        chosen = jnp.logical_or(chosen, hit)
        cur = jnp.where(hit, -jnp.inf, cur)
        idx_rows.append(ei)
        score_rows.append(jnp.sum(jnp.where(hit, scores, 0.0), axis=0, keepdims=True))
    top_idx = jnp.concatenate(idx_rows, axis=0)
    top_sc = jnp.concatenate(score_rows, axis=0)
    idx_ref[...] = top_idx
    wgt_ref[...] = top_sc / jnp.sum(top_sc, axis=0, keepdims=True) * ROUTED_SCALE

    chosen_b = chosen.astype(BF16)
    before = (_iota((tm, tm), 0) < _iota((tm, tm), 1)).astype(BF16)
    excl = jnp.dot(chosen_b, before, preferred_element_type=F32) + carry_ref[:, 0:1]
    rank_rows = [jnp.sum(jnp.where(e_idx == idx_rows[k], excl, 0.0), axis=0, keepdims=True)
                 for k in range(TOP_K)]
    rank_ref[...] = jnp.concatenate(rank_rows, axis=0).astype(I32)
    carry_ref[...] = carry_ref[...] + jnp.sum(chosen.astype(F32), axis=1, keepdims=True)
    cnt_ref[...] = carry_ref[...]


def _router(x, mod, w_router_t, router_bias):
    T, D = x.shape
    tm = ROUTER_TM
    E, K = N_EXPERTS, TOP_K
    bias = jnp.broadcast_to(router_bias.reshape(E, 1), (E, LANES))
    tok = lambda: pl.BlockSpec((K, tm), lambda i: (0, i))
    return pl.pallas_call(
        _router_kernel,
        grid=(T // tm,),
        in_specs=[pl.BlockSpec((tm, D), lambda i: (i, 0)),
                  pl.BlockSpec((1, D), lambda i: (0, 4)),
                  pl.BlockSpec((1, D), lambda i: (0, 3)),
                  pl.BlockSpec((E, D), lambda i: (0, 0)),
                  pl.BlockSpec((E, LANES), lambda i: (0, 0))],
        out_specs=[tok(), tok(), tok(), pl.BlockSpec((E, LANES), lambda i: (0, 0))],
        out_shape=[jax.ShapeDtypeStruct((K, T), I32), jax.ShapeDtypeStruct((K, T), F32),
                   jax.ShapeDtypeStruct((K, T), I32), jax.ShapeDtypeStruct((E, LANES), F32)],
        scratch_shapes=[pltpu.VMEM((E, LANES), F32)],
        compiler_params=_cparams(("arbitrary",)),
        name="router",
    )(x, mod, mod, w_router_t, bias)


def _plan_kernel(cnt_ref, idx_ref, rank_ref, pos_ref, texp_ref, end_ref, *, tile, n_tiles):
    E = N_EXPERTS
    tm = idx_ref.shape[1]
    cnt = cnt_ref[...]
    ntile = jnp.floor((cnt + (tile - 1)) * (1.0 / tile))
    lower = (_iota((E, E), 0) >= _iota((E, E), 1)).astype(F32)
    tile_end = jnp.dot(lower, ntile, preferred_element_type=F32)
    row_start = (tile_end - ntile) * tile
    e_idx = _iota((E, tm), 0)
    start_col = row_start[:, 0:1]
    for k in range(TOP_K):
        base = jnp.sum(jnp.where(e_idx == idx_ref[k:k + 1, :], start_col, 0.0), axis=0, keepdims=True)
        pos_ref[k:k + 1, :] = base.astype(I32) + rank_ref[k:k + 1, :]
    tile_i = _iota((E, n_tiles), 1).astype(F32)
    texp = jnp.sum((tile_end[:, 0:1] <= tile_i).astype(F32), axis=0, keepdims=True)
    texp_ref[...] = jnp.minimum(texp, E - 1.0).astype(I32)
    end_ref[...] = (tile_end * tile).astype(I32)


def _plan(cnt, idx_t, rank_t, tile, n_tiles):
    K, T = idx_t.shape
    E = N_EXPERTS
    tm = min(2048, T)
    tok = lambda: pl.BlockSpec((K, tm), lambda i: (0, i))
    return pl.pallas_call(
        functools.partial(_plan_kernel, tile=tile, n_tiles=n_tiles),
        grid=(T // tm,),
        in_specs=[pl.BlockSpec((E, LANES), lambda i: (0, 0)), tok(), tok()],
        out_specs=[tok(), pl.BlockSpec((1, n_tiles), lambda i: (0, 0)),
                   pl.BlockSpec((E, LANES), lambda i: (0, 0))],
        out_shape=[jax.ShapeDtypeStruct((K, T), I32), jax.ShapeDtypeStruct((1, n_tiles), I32),
                   jax.ShapeDtypeStruct((E, LANES), I32)],
        compiler_params=_cparams(("arbitrary",)),
        name="route_plan",
    )(cnt, idx_t, rank_t)


def _invert_kernel(pos_ref, init_ref, inv_ref, sem, *, n_pairs):
    fill = pltpu.make_async_copy(init_ref, inv_ref, sem)
    fill.start()
    fill.wait()

    def scatter(q, c):
        inv_ref[pos_ref[q]] = q
        return c

    lax.fori_loop(0, n_pairs, scatter, 0, unroll=8)


def _invert(pos_flat, inv_init):
    n_pairs = pos_flat.shape[0]
    return pl.pallas_call(
        functools.partial(_invert_kernel, n_pairs=n_pairs),
        in_specs=[pl.BlockSpec(memory_space=pltpu.SMEM), pl.BlockSpec(memory_space=pl.ANY)],
        out_specs=pl.BlockSpec(memory_space=pltpu.SMEM),
        out_shape=jax.ShapeDtypeStruct(inv_init.shape, I32),
        scratch_shapes=[pltpu.SemaphoreType.DMA(())],
        name="route_invert",
    )(pos_flat, inv_init)


def _expert_kernel(texp_ref, used_ref, inv_ref, h_ref, w1_ref, w3_ref, w2_ref, ys_ref,
                   xbuf, obuf, gsem, ssem, *, tm, n_tok, n_tiles):
    i = pl.program_id(0)
    used = used_ref[0]
    slot = lax.rem(i, 2)
    other = 1 - slot
    RT = ROW_TILE
    W = RT * LANES
    F = w1_ref.shape[2]

    def gather_start(tile_idx, s, rows):
        for j in rows:
            tok = inv_ref[tile_idx * tm + j] & (n_tok - 1)
            src = h_ref.at[pl.ds(pl.multiple_of(tok * RT, RT), RT)]
            pltpu.make_async_copy(src, xbuf.at[s, pl.ds(j * RT, RT)], gsem.at[s]).start()

    def gather_wait(s):
        pltpu.make_async_copy(h_ref.at[pl.ds(0, tm * RT)], xbuf.at[s], gsem.at[s]).wait()

    def scatter_start(tile_idx, s, rows):
        for j in rows:
            dst = ys_ref.at[pl.ds(pl.multiple_of(inv_ref[tile_idx * tm + j] * RT, RT), RT)]
            pltpu.make_async_copy(obuf.at[s, pl.ds(j * RT, RT)], dst, ssem.at[s]).start(priority=1)

    def scatter_wait(s):
        pltpu.make_async_copy(obuf.at[s], ys_ref.at[pl.ds(0, tm * RT)], ssem.at[s]).wait()

    @pl.when(i == 0)
    def _():
        obuf[1] = jnp.zeros((tm * RT, LANES), obuf.dtype)
        gather_start(0, 0, range(tm))

    @pl.when(jnp.logical_and(i >= 1, i < used))
    def _():
        scatter_wait(slot)

    @pl.when(i < used)
    def _():
        gather_wait(slot)
        nxt = jnp.minimum(i + 1, used - 1)
        prv = jnp.where(i == 0, n_tiles, i - 1)
        chunks = EXPERT_K_CHUNKS // 2
        per = tm // (4 * chunks)
        kc = W // chunks

        def dma_group(gi):
            rows = range(gi * per, (gi + 1) * per)
            gather_start(nxt, other, rows)
            scatter_start(prv, other, rows)

        h1 = jnp.zeros((tm, F), F32)
        h3 = jnp.zeros((tm, F), F32)
        for c in range(chunks):
            words = jnp.concatenate([xbuf[slot, pl.ds(w, tm, stride=RT), :]
                                     for w in range(c * kc // LANES, (c + 1) * kc // LANES)], axis=1)
            parts = _unpack_bf16_pair(words)
            for p, (part, k0) in enumerate(zip(parts, (c * kc, W + c * kc))):
                dma_group(2 * c + p)
                xc = part.astype(BF16)
                h1 = h1 + jnp.dot(xc, w1_ref[0, k0:k0 + kc, :].astype(BF16), preferred_element_type=F32)
                h3 = h3 + jnp.dot(xc, w3_ref[0, k0:k0 + kc, :].astype(BF16), preferred_element_type=F32)
        a = (_silu(h1) * h3).astype(BF16)
        for c in range(chunks):
            halves = []
            for p, n0 in enumerate((c * kc, W + c * kc)):
                dma_group(2 * chunks + 2 * c + p)
                halves.append(jnp.dot(a, w2_ref[0, :, n0:n0 + kc].astype(BF16), preferred_element_type=F32))
            words = _pack_bf16_pair(*halves)
            for n, w in enumerate(range(c * kc // LANES, (c + 1) * kc // LANES)):
                obuf[slot, pl.ds(w, tm, stride=RT), :] = words[:, n * LANES:(n + 1) * LANES]

    @pl.when(i == used - 1)
    def _():
        scatter_wait(other)
        scatter_start(i, slot, range(tm))
        gather_wait(other)
        scatter_wait(slot)


def _experts(tile_expert, tiles_used, inv, h2, w1, w3, w2, layer):
    RT = ROW_TILE
    T = h2.shape[0] // RT
    D, F = w1.shape[-2:]
    assert h2.shape[1] == LANES and 2 * RT * LANES == D
    tm = EXPERT_TM
    assert T & (T - 1) == 0
    n_tiles = inv.shape[0] // tm - 1
    wspec = lambda a, b: pl.BlockSpec((None, 1, a, b), lambda i, te, nu, iv: (layer, te[i], 0, 0))
    return pl.pallas_call(
        functools.partial(_expert_kernel, tm=tm, n_tok=T, n_tiles=n_tiles),
        grid_spec=pltpu.PrefetchScalarGridSpec(
            num_scalar_prefetch=3,
            grid=(n_tiles,),
            in_specs=[pl.BlockSpec(memory_space=pl.ANY), wspec(D, F), wspec(D, F), wspec(F, D)],
            out_specs=pl.BlockSpec(memory_space=pl.ANY),
            scratch_shapes=[pltpu.VMEM((2, tm * RT, LANES), I32), pltpu.VMEM((2, tm * RT, LANES), I32),
                            pltpu.SemaphoreType.DMA((2,)), pltpu.SemaphoreType.DMA((2,))]),
        out_shape=jax.ShapeDtypeStruct(((T * TOP_K + tm) * RT, LANES), I32),
        compiler_params=_cparams(("arbitrary",)),
        name="moe_experts",
    )(tile_expert, tiles_used, inv, h2, w1, w3, w2)


def _combine_kernel(*refs, alpha):
    y_refs = refs[:TOP_K]
    (x_ref, h_ref, g_ref, s1_ref, s3_ref, s2_ref, gt_ref, lw_ref, lb_ref, sc_ref, sh_ref,
     xo_ref, ho_ref) = refs[TOP_K:]
    tc = x_ref.shape[0]
    RT = ROW_TILE
    W = RT * LANES
    row_words = lambda ref: jnp.concatenate([ref[pl.ds(w, tc, stride=RT), :] for w in range(RT)], axis=1)
    h_lo, h_hi = (p.astype(BF16) for p in _unpack_bf16_pair(row_words(h_ref)))
    up = lambda w_ref: (jnp.dot(h_lo, w_ref[:W, :], preferred_element_type=F32)
                        + jnp.dot(h_hi, w_ref[W:, :], preferred_element_type=F32))
    a = _silu(up(s1_ref)) * up(s3_ref)
    shared = jnp.dot(a.astype(BF16), s2_ref[...], preferred_element_type=F32)
    g = g_ref[...]
    y_lo, y_hi = shared[:, :W], shared[:, W:]
    for k in range(TOP_K):
        e_lo, e_hi = _unpack_bf16_pair(row_words(y_refs[k]))
        y_lo = y_lo + g[:, k:k + 1] * e_lo
        y_hi = y_hi + g[:, k:k + 1] * e_hi
    y = jnp.concatenate([y_lo, y_hi], axis=1)
    r = alpha * x_ref[...] + (1.0 + gt_ref[...]) * y
    xn = _layer_norm_rows(r, lw_ref[...], lb_ref[...])
    xo_ref[...] = xn
    ho_ref[...] = (xn * (1.0 + sc_ref[...]) + sh_ref[...]).astype(ho_ref.dtype)


def _combine(ys, x, h2, gates, s1, s3, s2, mod, ln_w, ln_b, mod_next, alpha):
    T, D = x.shape
    F = s1.shape[-1]
    tc = COMBINE_TC
    nb = T // tc
    row = lambda idx: pl.BlockSpec((1, D), lambda i: (0, idx))
    full = lambda a, b: pl.BlockSpec((a, b), lambda i: (0, 0))
    tok = lambda: pl.BlockSpec((tc, D), lambda i: (i, 0))
    pair = lambda k: pl.BlockSpec((tc * ROW_TILE, LANES), lambda i: (k * nb + i, 0))
    return pl.pallas_call(
        functools.partial(_combine_kernel, alpha=alpha),
        grid=(nb,),
        in_specs=[pair(k) for k in range(TOP_K)] + [
            tok(), pair(0), pl.BlockSpec((tc, TOP_K), lambda i: (i, 0)),
            full(D, F), full(D, F), full(F, D), row(5), row(0), row(0), row(1), row(0)],
        out_specs=[tok(), tok()],
        out_shape=[jax.ShapeDtypeStruct((T, D), F32), jax.ShapeDtypeStruct((T, D), BF16)],
        compiler_params=_cparams(("arbitrary",)),
        name="moe_combine",
    )(*([ys] * TOP_K), x, h2, gates, s1, s3, s2, mod, ln_w.reshape(1, D), ln_b.reshape(1, D),
      mod_next, mod_next)


def kernel(x, c, w_ada, b_ada, w_in, conv_w, gdn_a_log, gdn_dt_bias, gdn_norm_w, sg_norm_w, sg_norm_b, sg_w, sg_b, w_out, ln1_w, ln1_b, w_router, router_bias, exp_w1, exp_w3, exp_w2, sh_w1, sh_w3, sh_w2, ln2_w, ln2_b):
    B, T, D = x.shape
    assert B == 1
    L = w_ada.shape[0]
    alpha = (2.0 * L) ** 0.25
    main_cols = 3 * SB_W + 4 * GDN_W
    assert main_cols % 512 == 0 and w_in.shape[2] == main_cols + 2 * GDN_HEADS + 2 * SG_W
    n_tiles = (T * TOP_K) // EXPERT_TM + N_EXPERTS
    n_rows = n_tiles * EXPERT_TM
    n_tiles_pad = -(-n_tiles // LANES) * LANES

    xt = x.reshape(T, D)
    mod_all = _ada_mod(c, w_ada, b_ada)

    ab_cols = w_in[:, :, main_cols:main_cols + 2 * GDN_HEADS]
    ab_cols = jnp.pad(ab_cols, ((0, 0), (0, 0), (0, LANES - 2 * GDN_HEADS)))
    w_tail = jnp.concatenate([w_in[:, :, main_cols + 2 * GDN_HEADS:], ab_cols], axis=-1)
    w_out_b = w_out.astype(BF16)
    w_router_t = jnp.swapaxes(w_router, 1, 2)
    s1, s3, s2 = sh_w1.astype(BF16), sh_w3.astype(BF16), sh_w2.astype(BF16)

    inv_init = T * TOP_K + lax.rem(jnp.arange(n_rows + EXPERT_TM, dtype=I32), EXPERT_TM)

    h = _modulate(xt, mod_all[0], 1, 0)
    for l in range(L):
        mod = mod_all[l]
        mod_next = mod_all[min(l + 1, L - 1)]
        sb_qkv = _proj(h, w_in, l, 0, 3 * SB_W // 512, 512, BF16)
        proj_g = _proj(h, w_in, l, 3 * SB_W // 512, 4 * GDN_W // 512, 512, F32)
        tail = _proj(h, w_tail, l, 0, 1, w_tail.shape[2], F32)
        y_sb = _sb_attention(sb_qkv)
        y_gdn = _gdn(proj_g, tail, conv_w[l], gdn_a_log[l], gdn_dt_bias[l], gdn_norm_w[l])
        y_sg = _spatial_gating(tail, sg_norm_w[l], sg_norm_b[l], sg_w[l], sg_b[l])
        xt, h2 = _outproj_ln(xt, y_sb, y_gdn, y_sg, w_out_b[l], mod, ln1_w[l], ln1_b[l], alpha)
        idx_t, wgt_t, rank_t, cnt = _router(xt, mod, w_router_t[l], router_bias[l])
        pos_t, tile_expert, seg_end = _plan(cnt, idx_t, rank_t, EXPERT_TM, n_tiles_pad)
        inv = _invert(pos_t.reshape(-1), inv_init)
        tiles_used = seg_end[N_EXPERTS - 1:, 0] // EXPERT_TM
        ys = _experts(tile_expert.reshape(-1), tiles_used, inv, h2, exp_w1, exp_w3, exp_w2, l)
        xt, h = _combine(ys, xt, h2, wgt_t.T, s1[l], s3[l], s2[l], mod, ln2_w[l], ln2_b[l],
                         mod_next, alpha)
    return xt.reshape(B, T, D)
```
